```python
import math
import jax, jax.numpy as jnp
from jax import lax
import numpy as np

D_MODEL = 4096
BATCH = 4
SEQ = 2048
DEPTH = 4
DEC_BATCH = 128
DEC_SEQ = 1
PAST_LEN = 16384
PAGE_SIZE = 128

N_META = 16
H_A = D_MODEL // 256
DK = 128
DV = 128
KEY_W = H_A * DK
VAL_W = H_A * DV
QKV_W = 2 * KEY_W + VAL_W
CONV_A = 4
CHUNK = 64
W_B = D_MODEL // 2
CONV_B = 3
D_FF = 256 * ((8 * D_MODEL // 3 + 255) // 256)
CONV_F = 3
EPS = 1e-6
N_IN = QKV_W + 2 * H_A + VAL_W + 3 * W_B + 2 * D_MODEL

kernel_name = 'hybrid_gdn_shortconv_convffn_decode_step'


def _rmsnorm(x, g):
    xf = x.astype(jnp.float32)
    y = xf * lax.rsqrt(jnp.mean(xf * xf, axis=-1, keepdims=True) + EPS)
    return (y * g.astype(jnp.float32)).astype(x.dtype)


def _l2norm(x):
    return x * lax.rsqrt(jnp.sum(x * x, axis=-1, keepdims=True) + EPS)


def _causal_conv(x_full, w):
    K = w.shape[0]
    L = x_full.shape[1] - K + 1
    acc = x_full[:, K - 1:K - 1 + L] * w[K - 1]
    for i in range(K - 1):
        acc = acc + x_full[:, i:i + L] * w[i]
    return acc


def _delta_chunk(S, q, k, v, g, beta):
    C = q.shape[2]
    G = jnp.cumsum(g, axis=-1)
    idx = jnp.arange(C)
    incl = idx[:, None] >= idx[None, :]
    strict = idx[:, None] > idx[None, :]
    decay = jnp.exp(jnp.where(incl, G[..., :, None] - G[..., None, :], -jnp.inf))
    gam = jnp.exp(G)[..., None]
    kk = jnp.einsum('bhid,bhjd->bhij', k, k)
    lower = jnp.where(strict, beta[..., :, None] * kk * decay, 0.0)
    a_mat = lower + jnp.eye(C, dtype=lower.dtype)
    rhs = beta[..., None] * (v - gam * jnp.einsum('bhck,bhkv->bhcv', k, S))
    u = lax.linalg.triangular_solve(a_mat, rhs, left_side=True, lower=True, unit_diagonal=True)
    qk = jnp.einsum('bhid,bhjd->bhij', q, k) * decay
    o = gam * jnp.einsum('bhck,bhkv->bhcv', q, S) + jnp.einsum('bhij,bhjv->bhiv', qk, u)
    g_last = G[..., -1]
    w_tail = jnp.exp(g_last[..., None] - G)
    S_new = jnp.exp(g_last)[..., None, None] * S + jnp.einsum('bhck,bhcv->bhkv', k * w_tail[..., None], u)
    return S_new, o


def _delta_scan(S, q, k, v, g, beta, C):
    B, H, L = g.shape
    n = L // C

    def to_chunks(t):
        return jnp.moveaxis(t.reshape((B, H, n, C) + t.shape[3:]), 2, 0)

    xs = tuple(to_chunks(t) for t in (q, k, v, g, beta))
    S, o = lax.scan(lambda s, c: _delta_chunk(s, *c), S, xs)
    o = jnp.moveaxis(o, 0, 2).reshape(B, H, L, DV)
    return S, o


def _layer(x, buf_qkv, buf_mix, buf_ffn, S0, n_lead,
           norm1_g, w_in, conv_qkv_w, a_log, dt_bias, onorm_g, conv_mix_w,
           w_up_a, w_up_b, w_o, norm2_g, w_ffn_up, conv_ffn_w, conv_ffn_b, w_ffn_down):
    B, L, _ = x.shape
    dt = x.dtype
    f32 = jnp.float32
    h = _rmsnorm(x, norm1_g)
    z = h @ w_in
    sizes = [QKV_W, H_A, H_A, VAL_W, W_B, W_B, W_B, D_MODEL]
    qkv_pre, a_raw, b_raw, g_out, b_gate, c_gate, h_b, z_a, z_b = jnp.split(z, list(np.cumsum(sizes)), axis=-1)

    qkv_full = jnp.concatenate([buf_qkv.astype(dt), qkv_pre], axis=1)
    qkv = jax.nn.silu(_causal_conv(qkv_full, conv_qkv_w))
    new_buf_qkv = qkv_full[:, -(CONV_A - 1):]
    q, k, v = jnp.split(qkv.astype(f32), [KEY_W, 2 * KEY_W], axis=-1)
    q = _l2norm(q.reshape(B, L, H_A, DK)) * (DK ** -0.5)
    k = _l2norm(k.reshape(B, L, H_A, DK))
    v = v.reshape(B, L, H_A, DV)
    beta = jax.nn.sigmoid(b_raw.astype(f32))
    g = -jnp.exp(a_log.astype(f32)) * jax.nn.softplus(a_raw.astype(f32) + dt_bias.astype(f32))
    q, k, v, g, beta = (jnp.swapaxes(t, 1, 2) for t in (q, k, v, g, beta))
    S = S0.astype(f32)
    S, o = _delta_chunk(S, *(t[:, :, :n_lead] for t in (q, k, v, g, beta)))
    if L > n_lead:
        S, o_rest = _delta_scan(S, *(t[:, :, n_lead:] for t in (q, k, v, g, beta)), CHUNK)
        o = jnp.concatenate([o, o_rest], axis=2)
    o = jnp.swapaxes(o, 1, 2)
    o = _rmsnorm(o, onorm_g) * jax.nn.silu(g_out.astype(f32).reshape(B, L, H_A, DV))
    o = o.reshape(B, L, VAL_W).astype(dt)

    yb_pre = c_gate * h_b
    mix_full = jnp.concatenate([buf_mix.astype(dt), yb_pre], axis=1)
    y_b = b_gate * _causal_conv(mix_full, conv_mix_w)
    new_buf_mix = mix_full[:, -(CONV_B - 1):]

    merged = jax.nn.sigmoid(z_a) * (o @ w_up_a) + jax.nn.sigmoid(z_b) * (y_b @ w_up_b)
    x = x + merged @ w_o

    h2 = _rmsnorm(x, norm2_g)
    gate, val = jnp.split(h2 @ w_ffn_up, 2, axis=-1)
    ffn_full = jnp.concatenate([buf_ffn.astype(dt), gate], axis=1)
    gate_c = _causal_conv(ffn_full, conv_ffn_w) + conv_ffn_b
    new_buf_ffn = ffn_full[:, -(CONV_F - 1):]
    x = x + (jax.nn.silu(gate_c) * val) @ w_ffn_down
    return x, S, new_buf_qkv, new_buf_mix, new_buf_ffn


def setup_inputs(seed: int = 0) -> dict:
    key = jax.random.key(seed)
    ks = jax.random.split(key, 24)
    f32 = jnp.float32

    def nrm(k, shape, scale):
        return jax.random.normal(k, shape, f32) * scale

    x_prompt = nrm(ks[0], (BATCH, SEQ, D_MODEL), 1.0)
    x_sample = nrm(ks[1], (DEC_BATCH, DEC_SEQ, D_MODEL), 1.0)
    state_delta = nrm(ks[2], (DEPTH, DEC_BATCH, H_A, DK, DV), 0.1)
    state_conv_qkv = nrm(ks[3], (DEPTH, DEC_BATCH, CONV_A - 1, QKV_W), 1.0)
    state_conv_mix = nrm(ks[4], (DEPTH, DEC_BATCH, CONV_B - 1, W_B), 0.5)
    state_conv_ffn = nrm(ks[5], (DEPTH, DEC_BATCH, CONV_F - 1, D_FF), 1.0)
    meta_tokens = nrm(ks[6], (N_META, D_MODEL), 1.0)
    norm1_g = 1.0 + nrm(ks[7], (DEPTH, D_MODEL), 0.02)
    w_in = nrm(ks[8], (DEPTH, D_MODEL, N_IN), D_MODEL ** -0.5)
    conv_qkv_w = nrm(ks[9], (DEPTH, CONV_A, QKV_W), CONV_A ** -0.5)
    a_log = jnp.log(jax.random.uniform(ks[10], (DEPTH, H_A), f32, 1.0, 16.0))
    dt0 = jnp.exp(jax.random.uniform(ks[11], (DEPTH, H_A), f32, math.log(1e-3), math.log(1e-1)))
    dt_bias = dt0 + jnp.log(-jnp.expm1(-dt0))
    onorm_g = 1.0 + nrm(ks[12], (DEPTH, DV), 0.02)
    conv_mix_w = nrm(ks[13], (DEPTH, CONV_B, W_B), CONV_B ** -0.5)
    w_up_a = nrm(ks[14], (DEPTH, VAL_W, D_MODEL), VAL_W ** -0.5)
    w_up_b = nrm(ks[15], (DEPTH, W_B, D_MODEL), W_B ** -0.5)
    w_o = nrm(ks[16], (DEPTH, D_MODEL, D_MODEL), D_MODEL ** -0.5)
    norm2_g = 1.0 + nrm(ks[17], (DEPTH, D_MODEL), 0.02)
    w_ffn_up = nrm(ks[18], (DEPTH, D_MODEL, 2 * D_FF), D_MODEL ** -0.5)
    conv_ffn_w = nrm(ks[19], (DEPTH, CONV_F, D_FF), CONV_F ** -0.5)
    conv_ffn_b = nrm(ks[20], (DEPTH, D_FF), 0.01)
    w_ffn_down = nrm(ks[21], (DEPTH, D_FF, D_MODEL), D_FF ** -0.5)
    final_g = 1.0 + nrm(ks[22], (D_MODEL,), 0.02)
    return {'x_prompt': x_prompt, 'x_sample': x_sample,
            'state_delta': state_delta, 'state_conv_qkv': state_conv_qkv,
            'state_conv_mix': state_conv_mix, 'state_conv_ffn': state_conv_ffn,
            'meta_tokens': meta_tokens, 'norm1_g': norm1_g, 'w_in': w_in,
            'conv_qkv_w': conv_qkv_w, 'a_log': a_log, 'dt_bias': dt_bias, 'onorm_g': onorm_g,
            'conv_mix_w': conv_mix_w, 'w_up_a': w_up_a, 'w_up_b': w_up_b, 'w_o': w_o,
            'norm2_g': norm2_g, 'w_ffn_up': w_ffn_up, 'conv_ffn_w': conv_ffn_w,
            'conv_ffn_b': conv_ffn_b, 'w_ffn_down': w_ffn_down, 'final_g': final_g}


def reference(x_prompt, x_sample, state_delta, state_conv_qkv, state_conv_mix, state_conv_ffn,
              meta_tokens, norm1_g, w_in, conv_qkv_w, a_log, dt_bias, onorm_g, conv_mix_w,
              w_up_a, w_up_b, w_o, norm2_g, w_ffn_up, conv_ffn_w, conv_ffn_b, w_ffn_down, final_g):
    B = x_prompt.shape[0]
    dt = x_prompt.dtype
    xp = jnp.concatenate([jnp.broadcast_to(meta_tokens[None].astype(dt), (B, N_META, D_MODEL)), x_prompt], axis=1)
    xs = x_sample
    n_new = x_sample.shape[1]
    zq = jnp.zeros((B, CONV_A - 1, QKV_W), dt)
    zm = jnp.zeros((B, CONV_B - 1, W_B), dt)
    zf = jnp.zeros((B, CONV_F - 1, D_FF), dt)
    zs = jnp.zeros((B, H_A, DK, DV), jnp.float32)
    sp, ss, qp, qs, mp, ms, fp, fs = [], [], [], [], [], [], [], []
    for l in range(DEPTH):
        pw = (norm1_g[l], w_in[l], conv_qkv_w[l], a_log[l], dt_bias[l], onorm_g[l], conv_mix_w[l],
              w_up_a[l], w_up_b[l], w_o[l], norm2_g[l], w_ffn_up[l], conv_ffn_w[l], conv_ffn_b[l], w_ffn_down[l])
        xp, s1, q1, m1, f1 = _layer(xp, zq, zm, zf, zs, N_META, *pw)
        xs, s2, q2, m2, f2 = _layer(xs, state_conv_qkv[l], state_conv_mix[l], state_conv_ffn[l],
                                    state_delta[l], n_new, *pw)
        sp.append(s1); ss.append(s2); qp.append(q1); qs.append(q2)
        mp.append(m1); ms.append(m2); fp.append(f1); fs.append(f2)
    y_prompt = _rmsnorm(xp, final_g)[:, N_META:]
    y_sample = _rmsnorm(xs, final_g)
    return (y_prompt, y_sample,
            jnp.stack(sp), jnp.stack(ss),
            jnp.stack(qp), jnp.stack(qs),
            jnp.stack(mp), jnp.stack(ms),
            jnp.stack(fp), jnp.stack(fs))
```

```python
import functools

import jax
import jax.numpy as jnp
from jax import lax
from jax.experimental import pallas as pl
from jax.experimental.pallas import tpu as pltpu

F32 = jnp.float32
BF16 = jnp.bfloat16
EPS = 1e-6
HIGHEST = lax.Precision.HIGHEST
CHUNK = 64
LANES = 128
SUBLANES = 8
MXU_COLS = 256


def _sigmoid(x):
    return 1.0 / (1.0 + jnp.exp(-x))


def _silu(x):
    return x * _sigmoid(x)


def _softplus(x):
    return jnp.maximum(x, 0.0) + jnp.log1p(jnp.exp(-jnp.abs(x)))


def _dot(a, b, precision=None):
    return jnp.dot(a, b, preferred_element_type=F32, precision=precision)


def _call(kernel, *, grid, in_specs, out_specs, out_shape, scratch_shapes=(), name=None,
          input_output_aliases=None):
    return pl.pallas_call(
        kernel, grid=grid, in_specs=in_specs, out_specs=out_specs, out_shape=out_shape,
        scratch_shapes=list(scratch_shapes), name=name,
        input_output_aliases=input_output_aliases or {},
        compiler_params=pltpu.CompilerParams(dimension_semantics=("arbitrary",) * len(grid)))


def _rmsnorm_kernel(x_ref, g_ref, o_ref):
    x = x_ref[...]
    ms = jnp.mean(x * x, axis=-1, keepdims=True)
    o_ref[...] = ((x * lax.rsqrt(ms + EPS)) * g_ref[...]).astype(o_ref.dtype)


def _rmsnorm(x2d, g_row, out_dtype, rows):
    m, d = x2d.shape
    return _call(
        _rmsnorm_kernel, grid=(m // rows,),
        in_specs=[pl.BlockSpec((rows, d), lambda i: (i, 0)), pl.BlockSpec((1, d), lambda i: (0, 0))],
        out_specs=pl.BlockSpec((rows, d), lambda i: (i, 0)),
        out_shape=jax.ShapeDtypeStruct((m, d), out_dtype), name="rmsnorm")(x2d, g_row)


def _mm_kernel(a_ref, w_ref, o_ref):
    o_ref[...] = _dot(a_ref[...], w_ref[...]).astype(o_ref.dtype)


def _mm(a, w_stack, layer, tm, tn, name):
    m, k = a.shape
    n = w_stack.shape[-1]
    return _call(
        _mm_kernel, grid=(m // tm, n // tn),
        in_specs=[pl.BlockSpec((tm, k), lambda i, j: (i, 0)),
                  pl.BlockSpec((None, k, tn), lambda i, j: (layer, 0, j))],
        out_specs=pl.BlockSpec((tm, tn), lambda i, j: (i, j)),
        out_shape=jax.ShapeDtypeStruct((m, n), F32), name=name)(a, w_stack)


def _mm_res_kernel(a_ref, w_ref, r_ref, o_ref):
    o_ref[...] = r_ref[...] + _dot(a_ref[...], w_ref[...])


def _mm_res(a, w_stack, layer, res, tm, tn, name):
    m, k = a.shape
    n = w_stack.shape[-1]
    return _call(
        _mm_res_kernel, grid=(m // tm, n // tn),
        in_specs=[pl.BlockSpec((tm, k), lambda i, j: (i, 0)),
                  pl.BlockSpec((None, k, tn), lambda i, j: (layer, 0, j)),
                  pl.BlockSpec((tm, tn), lambda i, j: (i, j))],
        out_specs=pl.BlockSpec((tm, tn), lambda i, j: (i, j)),
        out_shape=jax.ShapeDtypeStruct((m, n), F32), name=name)(a, w_stack, res)


def _l2norm_heads(x, dk, scale):
    parts = []
    for i in range(x.shape[1] // dk):
        p = x[:, i * dk:(i + 1) * dk]
        p = p * lax.rsqrt(jnp.sum(p * p, axis=-1, keepdims=True) + EPS)
        parts.append(p * scale if scale != 1.0 else p)
    return jnp.concatenate(parts, axis=-1)


def _qkv_finish(c, o_write, n, n_q, n_k, dk):
    @pl.when(n < n_q)
    def _():
        o_write(_l2norm_heads(c, dk, dk ** -0.5))

    @pl.when(jnp.logical_and(n >= n_q, n < n_q + n_k))
    def _():
        o_write(_l2norm_heads(c, dk, 1.0))

    @pl.when(n >= n_q + n_k)
    def _():
        o_write(c)


def _qkv_prompt_kernel(h_ref, w_ref, cw_ref, qkv_ref, nb_ref, zs_ref, *, seq, pad, taps, n_q, n_k, dk):
    n = pl.program_id(1)
    z = _dot(h_ref[...], w_ref[...])
    zs_ref[0:SUBLANES, :] = jnp.zeros((SUBLANES, z.shape[1]), F32)
    zs_ref[SUBLANES:SUBLANES + seq, :] = z
    nb_ref[...] = zs_ref[SUBLANES + seq - (taps - 1):SUBLANES + seq, :]
    acc = z * cw_ref[taps - 1:taps, :]
    for i in range(taps - 1):
        s = taps - 1 - i
        acc = acc + zs_ref[SUBLANES - s:SUBLANES - s + seq, :] * cw_ref[i:i + 1, :]
    c = _silu(acc)
    qkv_ref[0:pad, :] = jnp.zeros((pad, z.shape[1]), F32)

    def write(val):
        qkv_ref[pad:pad + seq, :] = val

    _qkv_finish(c, write, n, n_q, n_k, dk)


def _qkv_prompt(h3, w_stack, layer, cw, pad, dk, key_w):
    b, seq, d = h3.shape
    width = w_stack.shape[-1]
    taps = cw.shape[0]
    tn = MXU_COLS
    kern = functools.partial(_qkv_prompt_kernel, seq=seq, pad=pad, taps=taps,
                             n_q=key_w // tn, n_k=key_w // tn, dk=dk)
    return _call(
        kern, grid=(b, width // tn),
        in_specs=[pl.BlockSpec((None, seq, d), lambda i, j: (i, 0, 0), pipeline_mode=pl.Buffered(1)),
                  pl.BlockSpec((None, d, tn), lambda i, j: (layer, 0, j)),
                  pl.BlockSpec((taps, tn), lambda i, j: (0, j))],
        out_specs=[pl.BlockSpec((None, seq + pad, tn), lambda i, j: (i, 0, j)),
                   pl.BlockSpec((None, taps - 1, tn), lambda i, j: (i, 0, j))],
        out_shape=[jax.ShapeDtypeStruct((b, seq + pad, width), F32),
                   jax.ShapeDtypeStruct((b, taps - 1, width), F32)],
        scratch_shapes=[pltpu.VMEM((seq + SUBLANES, tn), F32)], name="qkv_prompt")(h3, w_stack, cw)


def _qkv_sample_kernel(h_ref, w_ref, cw_ref, st_ref, qkv_ref, z_ref, *, taps, n_q, n_k, dk):
    n = pl.program_id(0)
    z = _dot(h_ref[...], w_ref[...])
    z_ref[...] = z
    acc = z * cw_ref[taps - 1:taps, :]
    for i in range(taps - 1):
        acc = acc + st_ref[i] * cw_ref[i:i + 1, :]
    c = _silu(acc)

    def write(val):
        qkv_ref[...] = val

    _qkv_finish(c, write, n, n_q, n_k, dk)


def _qkv_sample(h2, w_stack, layer, cw, st_stack, dk, key_w):
    bs, d = h2.shape
    width = w_stack.shape[-1]
    taps = cw.shape[0]
    tn = MXU_COLS
    kern = functools.partial(_qkv_sample_kernel, taps=taps, n_q=key_w // tn, n_k=key_w // tn, dk=dk)
    return _call(
        kern, grid=(width // tn,),
        in_specs=[pl.BlockSpec((bs, d), lambda j: (0, 0)),
                  pl.BlockSpec((None, d, tn), lambda j: (layer, 0, j)),
                  pl.BlockSpec((taps, tn), lambda j: (0, j)),
                  pl.BlockSpec((None, taps - 1, bs, tn), lambda j: (layer, 0, 0, j))],
        out_specs=[pl.BlockSpec((bs, tn), lambda j: (0, j)), pl.BlockSpec((bs, tn), lambda j: (0, j))],
        out_shape=[jax.ShapeDtypeStruct((bs, width), F32), jax.ShapeDtypeStruct((bs, width), F32)],
        name="qkv_sample")(h2, w_stack, cw, st_stack)


def _gates_math(ab, alog_ref, dtb_ref):
    g = -jnp.exp(alog_ref[...]) * _softplus(ab + dtb_ref[...])
    return g, _sigmoid(ab)


def _expand_heads(x, first_head, lane_off, heads, dk):
    src = lax.broadcasted_iota(jnp.int32, (LANES, heads * dk), 0)
    dst = lax.broadcasted_iota(jnp.int32, (LANES, heads * dk), 1) // dk + (first_head + lane_off)
    return _dot(x, (src == dst).astype(F32), precision=HIGHEST)


def _gates_prompt_kernel(h_ref, wab_ref, alog_ref, dtb_ref, g_ref, b_ref, gp_ref, bp_ref,
                         *, seq, pad, n_heads, dk, heads_per_step):
    j = pl.program_id(1)
    ab = _dot(h_ref[...], wab_ref[...])
    g, beta = _gates_math(ab, alog_ref, dtb_ref)
    zeros = jnp.zeros((pad, LANES), F32)
    gp_ref[0:pad, :] = zeros
    bp_ref[0:pad, :] = zeros
    gp_ref[pad:pad + seq, :] = g
    bp_ref[pad:pad + seq, :] = beta
    row = lax.broadcasted_iota(jnp.int32, (CHUNK, CHUNK), 0)
    col = lax.broadcasted_iota(jnp.int32, (CHUNK, CHUNK), 1)
    tril = (row >= col).astype(F32)
    for c in range((seq + pad) // CHUNK):
        rows = slice(c * CHUNK, (c + 1) * CHUNK)
        gp_ref[rows, :] = _dot(tril, gp_ref[rows, :], precision=HIGHEST)
    g_ref[...] = _expand_heads(gp_ref[...], j * heads_per_step, 0, heads_per_step, dk)
    b_ref[...] = _expand_heads(bp_ref[...], j * heads_per_step, n_heads, heads_per_step, dk)


def _gates_prompt(h3, wab_stack, layer, alog_row, dtb_row, pad, n_heads, dk):
    b, seq, d = h3.shape
    t = seq + pad
    hps = 4
    kern = functools.partial(_gates_prompt_kernel, seq=seq, pad=pad, n_heads=n_heads, dk=dk, heads_per_step=hps)
    out = jax.ShapeDtypeStruct((b, t, n_heads * dk), F32)
    return _call(
        kern, grid=(b, n_heads // hps),
        in_specs=[pl.BlockSpec((None, seq, d), lambda i, j: (i, 0, 0), pipeline_mode=pl.Buffered(1)),
                  pl.BlockSpec((None, d, LANES), lambda i, j: (layer, 0, 0)),
                  pl.BlockSpec((1, LANES), lambda i, j: (0, 0)),
                  pl.BlockSpec((1, LANES), lambda i, j: (0, 0))],
        out_specs=[pl.BlockSpec((None, t, hps * dk), lambda i, j: (i, 0, j)),
                   pl.BlockSpec((None, t, hps * dk), lambda i, j: (i, 0, j))],
        out_shape=[out, out],
        scratch_shapes=[pltpu.VMEM((t, LANES), F32), pltpu.VMEM((t, LANES), F32)],
        name="gates_prompt")(h3, wab_stack, alog_row, dtb_row)


def _gates_sample_kernel(h_ref, wab_ref, alog_ref, dtb_ref, g_ref, b_ref, *, n_heads, dk):
    ab = _dot(h_ref[...], wab_ref[...])
    g, beta = _gates_math(ab, alog_ref, dtb_ref)
    g_ref[...] = _expand_heads(g, 0, 0, n_heads, dk)
    b_ref[...] = _expand_heads(beta, 0, n_heads, n_heads, dk)


def _gates_sample(h2, wab_stack, layer, alog_row, dtb_row, n_heads, dk):
    bs, d = h2.shape
    out = jax.ShapeDtypeStruct((bs, n_heads * dk), F32)
    return _call(
        functools.partial(_gates_sample_kernel, n_heads=n_heads, dk=dk), grid=(1,),
        in_specs=[pl.BlockSpec((bs, d), lambda i: (0, 0)),
                  pl.BlockSpec((None, d, LANES), lambda i: (layer, 0, 0)),
                  pl.BlockSpec((1, LANES), lambda i: (0, 0)),
                  pl.BlockSpec((1, LANES), lambda i: (0, 0))],
        out_specs=[pl.BlockSpec((bs, n_heads * dk), lambda i: (0, 0)),
                   pl.BlockSpec((bs, n_heads * dk), lambda i: (0, 0))],
        out_shape=[out, out], name="gates_sample")(h2, wab_stack, alog_row, dtb_row)


def _mixb_prompt_kernel(h_ref, wb_ref, wc_ref, wh_ref, cw_ref, y_ref, nb_ref, ps_ref, *, seq, taps):
    h = h_ref[...]
    pre = _dot(h, wc_ref[...]) * _dot(h, wh_ref[...])
    ps_ref[0:SUBLANES, :] = jnp.zeros((SUBLANES, pre.shape[1]), F32)
    ps_ref[SUBLANES:SUBLANES + seq, :] = pre
    nb_ref[...] = ps_ref[SUBLANES + seq - (taps - 1):SUBLANES + seq, :]
    acc = pre * cw_ref[taps - 1:taps, :]
    for i in range(taps - 1):
        s = taps - 1 - i
        acc = acc + ps_ref[SUBLANES - s:SUBLANES - s + seq, :] * cw_ref[i:i + 1, :]
    y_ref[...] = (_dot(h, wb_ref[...]) * acc).astype(y_ref.dtype)


def _mixb_prompt(h3, wb_stack, wc_stack, wh_stack, layer, cw):
    b, seq, d = h3.shape
    width = wb_stack.shape[-1]
    taps = cw.shape[0]
    tn = MXU_COLS
    wspec = pl.BlockSpec((None, d, tn), lambda i, j: (layer, 0, j))
    return _call(
        functools.partial(_mixb_prompt_kernel, seq=seq, taps=taps), grid=(b, width // tn),
        in_specs=[pl.BlockSpec((None, seq, d), lambda i, j: (i, 0, 0), pipeline_mode=pl.Buffered(1)),
                  wspec, wspec, wspec, pl.BlockSpec((taps, tn), lambda i, j: (0, j))],
        out_specs=[pl.BlockSpec((None, seq, tn), lambda i, j: (i, 0, j)),
                   pl.BlockSpec((None, taps - 1, tn), lambda i, j: (i, 0, j))],
        out_shape=[jax.ShapeDtypeStruct((b, seq, width), BF16),
                   jax.ShapeDtypeStruct((b, taps - 1, width), F32)],
        scratch_shapes=[pltpu.VMEM((seq + SUBLANES, tn), F32)],
        name="mixb_prompt")(h3, wb_stack, wc_stack, wh_stack, cw)


def _mixb_sample_kernel(h_ref, wb_ref, wc_ref, wh_ref, cw_ref, st_ref, y_ref, pre_ref, *, taps):
    h = h_ref[...]
    pre = _dot(h, wc_ref[...]) * _dot(h, wh_ref[...])
    pre_ref[...] = pre
    acc = pre * cw_ref[taps - 1:taps, :]
    for i in range(taps - 1):
        acc = acc + st_ref[i] * cw_ref[i:i + 1, :]
    y_ref[...] = (_dot(h, wb_ref[...]) * acc).astype(y_ref.dtype)


def _mixb_sample(h2, wb_stack, wc_stack, wh_stack, layer, cw, st_stack):
    bs, d = h2.shape
    width = wb_stack.shape[-1]
    taps = cw.shape[0]
    tn = MXU_COLS
    wspec = pl.BlockSpec((None, d, tn), lambda j: (layer, 0, j))
    return _call(
        functools.partial(_mixb_sample_kernel, taps=taps), grid=(width // tn,),
        in_specs=[pl.BlockSpec((bs, d), lambda j: (0, 0)), wspec, wspec, wspec,
                  pl.BlockSpec((taps, tn), lambda j: (0, j)),
                  pl.BlockSpec((None, taps - 1, bs, tn), lambda j: (layer, 0, 0, j))],
        out_specs=[pl.BlockSpec((bs, tn), lambda j: (0, j)), pl.BlockSpec((bs, tn), lambda j: (0, j))],
        out_shape=[jax.ShapeDtypeStruct((bs, width), BF16), jax.ShapeDtypeStruct((bs, width), F32)],
        name="mixb_sample")(h2, wb_stack, wc_stack, wh_stack, cw, st_stack)


def _bmm(a, b, precision=None):
    return jnp.einsum("cij,cjk->cik", a, b, preferred_element_type=F32, precision=precision)


def _bmm_nt(a, b):
    return jnp.einsum("cid,cjd->cij", a, b, preferred_element_type=F32)


def _unit_lower_inverse(lm, row, col, eye):
    blk16 = (row // 16) == (col // 16)
    blk32 = (row // 32) == (col // 32)
    ld = jnp.where(blk16, lm, 0.0)
    lo1 = jnp.where(jnp.logical_and(blk32, jnp.logical_not(blk16)), lm, 0.0)
    lo2 = jnp.where(blk32, 0.0, lm)
    x = eye.astype(F32) - ld
    m = _bmm(ld, ld, HIGHEST)
    x = x + _bmm(x, m, HIGHEST)
    m = _bmm(m, m, HIGHEST)
    x = x + _bmm(x, m, HIGHEST)
    m = _bmm(m, m, HIGHEST)
    x = x + _bmm(x, m, HIGHEST)
    x = x - _bmm(x, _bmm(lo1, x, HIGHEST), HIGHEST)
    x = x - _bmm(x, _bmm(lo2, x, HIGHEST), HIGHEST)
    return x


def _delta_prompt_kernel(q_ref, k_ref, v_ref, g_ref, b_ref, go_ref, og_ref, o_ref, s_ref,
                         w_s, u_s, kt_s, qk_s, oo_s, *, seq, pad, dk):
    t = seq + pad
    nc = t // CHUNK
    q3 = q_ref[...].reshape(nc, CHUNK, dk)
    k3 = k_ref[...].reshape(nc, CHUNK, dk)
    v3 = v_ref[...].reshape(nc, CHUNK, dk)
    g3 = g_ref[...].reshape(nc, CHUNK, dk)
    b3 = b_ref[...].reshape(nc, CHUNK, dk)
    row = lax.broadcasted_iota(jnp.int32, (CHUNK, CHUNK), 0)
    col = lax.broadcasted_iota(jnp.int32, (CHUNK, CHUNK), 1)
    eye = row == col
    incl = row >= col
    gi = g3[:, :, :CHUNK]
    gj = jnp.sum(jnp.where(eye, gi, 0.0), axis=1, keepdims=True)
    decay = jnp.where(incl, jnp.exp(jnp.where(incl, gi - gj, 0.0)), 0.0)
    lm = jnp.where(row > col, b3[:, :, :CHUNK] * _bmm_nt(k3, k3) * decay, 0.0)
    tinv = _unit_lower_inverse(lm, row, col, eye)
    gam = jnp.exp(g3)
    u_s[...] = _bmm(tinv, b3 * v3, HIGHEST).reshape(t, dk)
    w_s[...] = _bmm(tinv, b3 * gam * k3, HIGHEST).reshape(t, dk)
    qk_s[...] = (_bmm_nt(q3, k3) * decay).reshape(t, CHUNK)
    kt_s[...] = (k3 * jnp.exp(g3[:, CHUNK - 1:CHUNK, :] - g3)).reshape(t, dk)

    def body(c, s):
        r0 = pl.multiple_of(c * CHUNK, CHUNK)
        rows = pl.ds(r0, CHUNK)
        u = u_s[rows, :] - _dot(w_s[rows, :], s)
        o = jnp.exp(g_ref[rows, :]) * _dot(q_ref[rows, :], s) + _dot(qk_s[rows, :], u)
        oo_s[rows, :] = o
        s_scale = jnp.exp(g_ref[pl.ds(r0 + CHUNK - 1, 1), :])
        return s_scale * s + lax.dot_general(kt_s[rows, :], u, (((0,), (0,)), ((), ())),
                                             preferred_element_type=F32)

    s_ref[...] = lax.fori_loop(0, nc, body, jnp.zeros((dk, dk), F32))
    o = oo_s[pad:pad + seq, :]
    o = (o * lax.rsqrt(jnp.mean(o * o, axis=-1, keepdims=True) + EPS)) * og_ref[...]
    o_ref[...] = (o * _silu(go_ref[...])).astype(o_ref.dtype)


def _delta_prompt(qkv, gcb, bcb, gout3, og_row, pad, n_heads, dk):
    b, t, _ = qkv.shape
    seq = t - pad

    def col_spec(rows, off):
        return pl.BlockSpec((None, rows, dk), lambda i, j: (i, 0, j + off))

    kern = functools.partial(_delta_prompt_kernel, seq=seq, pad=pad, dk=dk)
    return _call(
        kern, grid=(b, n_heads),
        in_specs=[col_spec(t, 0), col_spec(t, n_heads), col_spec(t, 2 * n_heads),
                  col_spec(t, 0), col_spec(t, 0), col_spec(seq, 0),
                  pl.BlockSpec((1, dk), lambda i, j: (0, 0))],
        out_specs=[col_spec(seq, 0), pl.BlockSpec((None, None, dk, dk), lambda i, j: (i, j, 0, 0))],
        out_shape=[jax.ShapeDtypeStruct((b, seq, n_heads * dk), BF16),
                   jax.ShapeDtypeStruct((b, n_heads, dk, dk), F32)],
        scratch_shapes=[pltpu.VMEM((t, dk), F32), pltpu.VMEM((t, dk), F32), pltpu.VMEM((t, dk), F32),
                        pltpu.VMEM((t, CHUNK), F32), pltpu.VMEM((t, dk), F32)],
        name="delta_prompt")(qkv, qkv, qkv, gcb, bcb, gout3, og_row)


def _delta_sample_kernel(q_ref, k_ref, v_ref, g_ref, b_ref, go_ref, og_ref, s_ref, *rest, n_heads, dk):
    o_ref, sn_ref = rest[-2:]
    q = q_ref[...]
    k = k_ref[...]
    v = v_ref[...]
    kq = jnp.concatenate([k, q, jnp.zeros((LANES - 2 * n_heads, dk), F32)], axis=0)
    kq_t = kq.T
    qk = jnp.sum(q * k, axis=-1, keepdims=True)
    outs = []
    for h in range(n_heads):
        s = s_ref[h]
        a = jnp.exp(g_ref[:, h * dk:(h + 1) * dk])
        beta = b_ref[:, h * dk:(h + 1) * dk]
        kcol = kq_t[:, h:h + 1]
        qcol = kq_t[:, n_heads + h:n_heads + h + 1]
        ks = jnp.sum(kcol * s, axis=0, keepdims=True)
        qs = jnp.sum(qcol * s, axis=0, keepdims=True)
        u = beta * (v[h:h + 1, :] - a * ks)
        outs.append(a * qs + qk[h:h + 1, :] * u)
        sn_ref[h] = a * s + kcol * u
    o = jnp.concatenate(outs, axis=0)
    o = (o * lax.rsqrt(jnp.mean(o * o, axis=-1, keepdims=True) + EPS)) * og_ref[...]
    o_ref[...] = (o * _silu(go_ref[...])).astype(o_ref.dtype)


def _delta_sample(q3, k3, v3, gcb, bcb, gout3, og_row, state, layer, prev_out):
    bs, n_heads, dk = q3.shape
    depth = state.shape[0]
    hspec = pl.BlockSpec((None, n_heads, dk), lambda i: (i, 0, 0))
    rspec = pl.BlockSpec((None, 1, n_heads * dk), lambda i: (i, 0, 0))
    sspec = pl.BlockSpec((None, None, n_heads, dk, dk), lambda i: (layer, i, 0, 0, 0))
    in_specs = [hspec, hspec, hspec, rspec, rspec, hspec, pl.BlockSpec((1, dk), lambda i: (0, 0)), sspec]
    args = [q3, k3, v3, gcb.reshape(bs, 1, -1), bcb.reshape(bs, 1, -1), gout3, og_row, state]
    aliases = {}
    if prev_out is not None:
        aliases = {len(args): 1}
        in_specs.append(pl.BlockSpec(memory_space=pl.ANY))
        args.append(prev_out)
    return _call(
        functools.partial(_delta_sample_kernel, n_heads=n_heads, dk=dk), grid=(bs,),
        in_specs=in_specs, out_specs=[hspec, sspec],
        out_shape=[jax.ShapeDtypeStruct((bs, n_heads, dk), BF16),
                   jax.ShapeDtypeStruct((depth, bs, n_heads, dk, dk), F32)],
        input_output_aliases=aliases, name="delta_sample")(*args)


def _merge_kernel(o_ref, y_ref, h_ref, wa_ref, wb_ref, wza_ref, wzb_ref, m_ref):
    h = h_ref[...]
    za = _sigmoid(_dot(h, wza_ref[...]))
    zb = _sigmoid(_dot(h, wzb_ref[...]))
    m_ref[...] = (za * _dot(o_ref[...], wa_ref[...]) + zb * _dot(y_ref[...], wb_ref[...])).astype(m_ref.dtype)


def _merge(o2, y2, h2, wa_stack, wb_stack, wza_stack, wzb_stack, layer, tm):
    m, d = h2.shape
    ka = o2.shape[1]
    kb = y2.shape[1]
    n = wa_stack.shape[-1]
    tn = MXU_COLS

    def wspec(kk):
        return pl.BlockSpec((None, kk, tn), lambda i, j: (layer, 0, j))

    return _call(
        _merge_kernel, grid=(m // tm, n // tn),
        in_specs=[pl.BlockSpec((tm, ka), lambda i, j: (i, 0)), pl.BlockSpec((tm, kb), lambda i, j: (i, 0)),
                  pl.BlockSpec((tm, d), lambda i, j: (i, 0)), wspec(ka), wspec(kb), wspec(d), wspec(d)],
        out_specs=pl.BlockSpec((tm, tn), lambda i, j: (i, j)),
        out_shape=jax.ShapeDtypeStruct((m, n), BF16), name="merge")(
            o2, y2, h2, wa_stack, wb_stack, wza_stack, wzb_stack)


def _ffn_prompt_kernel(h_ref, wg_ref, wv_ref, cw_ref, cb_ref, a_ref, nb_ref, gs_ref, *, seq, taps):
    h = h_ref[...]
    gate = _dot(h, wg_ref[...])
    gs_ref[0:SUBLANES, :] = jnp.zeros((SUBLANES, gate.shape[1]), F32)
    gs_ref[SUBLANES:SUBLANES + seq, :] = gate
    nb_ref[...] = gs_ref[SUBLANES + seq - (taps - 1):SUBLANES + seq, :]
    acc = gate * cw_ref[taps - 1:taps, :]
    for i in range(taps - 1):
        s = taps - 1 - i
        acc = acc + gs_ref[SUBLANES - s:SUBLANES - s + seq, :] * cw_ref[i:i + 1, :]
    a_ref[...] = (_silu(acc + cb_ref[...]) * _dot(h, wv_ref[...])).astype(a_ref.dtype)


def _ffn_prompt(h3, wup_stack, layer, cw, cb_row):
    b, seq, d = h3.shape
    dff = cw.shape[1]
    taps = cw.shape[0]
    tn = MXU_COLS
    nt = dff // tn
    return _call(
        functools.partial(_ffn_prompt_kernel, seq=seq, taps=taps), grid=(b, nt),
        in_specs=[pl.BlockSpec((None, seq, d), lambda i, j: (i, 0, 0), pipeline_mode=pl.Buffered(1)),
                  pl.BlockSpec((None, d, tn), lambda i, j: (layer, 0, j)),
                  pl.BlockSpec((None, d, tn), lambda i, j: (layer, 0, j + nt)),
                  pl.BlockSpec((taps, tn), lambda i, j: (0, j)),
                  pl.BlockSpec((1, tn), lambda i, j: (0, j))],
        out_specs=[pl.BlockSpec((None, seq, tn), lambda i, j: (i, 0, j)),
                   pl.BlockSpec((None, taps - 1, tn), lambda i, j: (i, 0, j))],
        out_shape=[jax.ShapeDtypeStruct((b, seq, dff), BF16),
                   jax.ShapeDtypeStruct((b, taps - 1, dff), F32)],
        scratch_shapes=[pltpu.VMEM((seq + SUBLANES, tn), F32)],
        name="ffn_prompt")(h3, wup_stack, wup_stack, cw, cb_row)


def _ffn_sample_kernel(h_ref, wg_ref, wv_ref, cw_ref, cb_ref, st_ref, a_ref, g_ref, *, taps):
    h = h_ref[...]
    gate = _dot(h, wg_ref[...])
    g_ref[...] = gate
    acc = gate * cw_ref[taps - 1:taps, :]
    for i in range(taps - 1):
        acc = acc + st_ref[i] * cw_ref[i:i + 1, :]
    a_ref[...] = (_silu(acc + cb_ref[...]) * _dot(h, wv_ref[...])).astype(a_ref.dtype)


def _ffn_sample(h2, wup_stack, layer, cw, cb_row, st_stack):
    bs, d = h2.shape
    dff = cw.shape[1]
    taps = cw.shape[0]
    tn = MXU_COLS
    nt = dff // tn
    return _call(
        functools.partial(_ffn_sample_kernel, taps=taps), grid=(nt,),
        in_specs=[pl.BlockSpec((bs, d), lambda j: (0, 0)),
                  pl.BlockSpec((None, d, tn), lambda j: (layer, 0, j)),
                  pl.BlockSpec((None, d, tn), lambda j: (layer, 0, j + nt)),
                  pl.BlockSpec((taps, tn), lambda j: (0, j)),
                  pl.BlockSpec((1, tn), lambda j: (0, j)),
                  pl.BlockSpec((None, taps - 1, bs, tn), lambda j: (layer, 0, 0, j))],
        out_specs=[pl.BlockSpec((bs, tn), lambda j: (0, j)), pl.BlockSpec((bs, tn), lambda j: (0, j))],
        out_shape=[jax.ShapeDtypeStruct((bs, dff), BF16), jax.ShapeDtypeStruct((bs, dff), F32)],
        name="ffn_sample")(h2, wup_stack, wup_stack, cw, cb_row, st_stack)


def _pad_lanes(x2d):
    return jnp.pad(x2d, ((0, 0), (0, LANES - x2d.shape[1])))


def _roll_state(state_l, new_row):
    return jnp.concatenate([state_l[:, 1:], new_row[:, None]], axis=1)


def kernel(x_prompt, x_sample, state_delta, state_conv_qkv, state_conv_mix, state_conv_ffn, meta_tokens,
           norm1_g, w_in, conv_qkv_w, a_log, dt_bias, onorm_g, conv_mix_w, w_up_a, w_up_b, w_o, norm2_g,
           w_ffn_up, conv_ffn_w, conv_ffn_b, w_ffn_down, final_g):
    b, seq0, d = x_prompt.shape
    bs = x_sample.shape[0]
    depth, _, n_heads, dk, dv = state_delta.shape
    assert dk == dv == LANES and x_sample.shape[1] == 1
    n_meta = meta_tokens.shape[0]
    seq = seq0 + n_meta
    pad = (-seq) % CHUNK
    key_w = n_heads * dk
    val_w = n_heads * dv
    qkv_w = state_conv_qkv.shape[-1]
    w_b = state_conv_mix.shape[-1]
    d_ff = state_conv_ffn.shape[-1]
    assert 2 * n_heads <= LANES

    o_ab = qkv_w
    o_go = o_ab + 2 * n_heads
    o_bg = o_go + val_w
    o_cg = o_bg + w_b
    o_hb = o_cg + w_b
    o_za = o_hb + w_b
    o_zb = o_za + d
    assert o_zb + d == w_in.shape[-1]

    def seg(lo, hi):
        return w_in[:, :, lo:hi].astype(BF16)

    w_qkv, w_go, w_bg, w_cg, w_hb, w_za, w_zb = (seg(0, o_ab), seg(o_go, o_bg), seg(o_bg, o_cg), seg(o_cg, o_hb),
                                                 seg(o_hb, o_za), seg(o_za, o_zb), seg(o_zb, o_zb + d))
    w_ab = jnp.pad(w_in[:, :, o_ab:o_go], ((0, 0), (0, 0), (0, LANES - 2 * n_heads))).astype(BF16)
    wua, wub, wo, wfu, wfd = (w.astype(BF16) for w in (w_up_a, w_up_b, w_o, w_ffn_up, w_ffn_down))
    st_qkv = jnp.swapaxes(state_conv_qkv, 1, 2)
    st_mix = jnp.swapaxes(state_conv_mix, 1, 2)
    st_ffn = jnp.swapaxes(state_conv_ffn, 1, 2)

    xp = jnp.concatenate([jnp.broadcast_to(meta_tokens[None], (b, n_meta, d)), x_prompt], axis=1)
    xp = xp.reshape(b * seq, d)
    xs = x_sample.reshape(bs, d)
    m = b * seq
    tm = seq // 3
    rows_norm = 192
    assert seq % 3 == 0 and tm % 16 == 0 and m % rows_norm == 0
    tn_o = 512
    tn_f = MXU_COLS

    sp, qp, qs, mp, ms, fp, fs = [], [], [], [], [], [], []
    s_sample = None
    for l in range(depth):
        alog_row = _pad_lanes(a_log[l][None])
        dtb_row = _pad_lanes(dt_bias[l][None])
        og_row = onorm_g[l][None]
        cb_row = conv_ffn_b[l][None]

        h = _rmsnorm(xp, norm1_g[l][None], BF16, rows_norm)
        h3 = h.reshape(b, seq, d)
        qkv, nb_q = _qkv_prompt(h3, w_qkv, l, conv_qkv_w[l], pad, dk, key_w)
        gcb, bcb = _gates_prompt(h3, w_ab, l, alog_row, dtb_row, pad, n_heads, dk)
        gout = _mm(h, w_go, l, tm, 1024, "gout")
        y_b, nb_m = _mixb_prompt(h3, w_bg, w_cg, w_hb, l, conv_mix_w[l])
        o, s_new = _delta_prompt(qkv, gcb, bcb, gout.reshape(b, seq, val_w), og_row, pad, n_heads, dk)
        merged = _merge(o.reshape(m, val_w), y_b.reshape(m, w_b), h, wua, wub, w_za, w_zb, l, tm)
        xp = _mm_res(merged, wo, l, xp, tm, tn_o, "out_proj")
        h2 = _rmsnorm(xp, norm2_g[l][None], BF16, rows_norm)
        act, nb_f = _ffn_prompt(h2.reshape(b, seq, d), wfu, l, conv_ffn_w[l], cb_row)
        xp = _mm_res(act.reshape(m, d_ff), wfd, l, xp, tm, tn_f, "ffn_down")
        sp.append(s_new); qp.append(nb_q); mp.append(nb_m); fp.append(nb_f)

        hs = _rmsnorm(xs, norm1_g[l][None], BF16, bs)
        qkv_s, z_new = _qkv_sample(hs, w_qkv, l, conv_qkv_w[l], st_qkv, dk, key_w)
        gcb_s, bcb_s = _gates_sample(hs, w_ab, l, alog_row, dtb_row, n_heads, dk)
        gout_s = _mm(hs, w_go, l, bs, 1024, "gout")
        yb_s, pre_new = _mixb_sample(hs, w_bg, w_cg, w_hb, l, conv_mix_w[l], st_mix)
        q3 = qkv_s[:, :key_w].reshape(bs, n_heads, dk)
        k3 = qkv_s[:, key_w:2 * key_w].reshape(bs, n_heads, dk)
        v3 = qkv_s[:, 2 * key_w:].reshape(bs, n_heads, dv)
        o_s, s_sample = _delta_sample(q3, k3, v3, gcb_s, bcb_s, gout_s.reshape(bs, n_heads, dv), og_row,
                                      state_delta, l, s_sample)
        merged_s = _merge(o_s.reshape(bs, val_w), yb_s, hs, wua, wub, w_za, w_zb, l, bs)
        xs = _mm_res(merged_s, wo, l, xs, bs, tn_o, "out_proj")
        h2s = _rmsnorm(xs, norm2_g[l][None], BF16, bs)
        act_s, gate_new = _ffn_sample(h2s, wfu, l, conv_ffn_w[l], cb_row, st_ffn)
        xs = _mm_res(act_s, wfd, l, xs, bs, tn_f, "ffn_down")
        qs.append(_roll_state(state_conv_qkv[l], z_new))
        ms.append(_roll_state(state_conv_mix[l], pre_new))
        fs.append(_roll_state(state_conv_ffn[l], gate_new))

    y_prompt = _rmsnorm(xp, final_g[None], F32, rows_norm).reshape(b, seq, d)[:, n_meta:]
    y_sample = _rmsnorm(xs, final_g[None], F32, bs).reshape(bs, 1, d)
    return (y_prompt, y_sample, jnp.stack(sp), s_sample, jnp.stack(qp), jnp.stack(qs),
            jnp.stack(mp), jnp.stack(ms), jnp.stack(fp), jnp.stack(fs))
```

```python
import functools

import jax
import jax.numpy as jnp
from jax import lax
from jax.experimental import pallas as pl
from jax.experimental.pallas import tpu as pltpu

F32 = jnp.float32
BF16 = jnp.bfloat16
EPS = 1e-6
HIGHEST = lax.Precision.HIGHEST
CHUNK = 64
LANES = 128
SUBLANES = 8
MXU_COLS = 256


def _sigmoid(x):
    return 1.0 / (1.0 + jnp.exp(-x))


def _silu(x):
    return x * _sigmoid(x)


def _softplus(x):
    return jnp.maximum(x, 0.0) + jnp.log1p(jnp.exp(-jnp.abs(x)))


def _dot(a, b, precision=None):
    return jnp.dot(a, b, preferred_element_type=F32, precision=precision)


def _call(kernel, *, grid, in_specs, out_specs, out_shape, scratch_shapes=(), name=None,
          input_output_aliases=None):
    return pl.pallas_call(
        kernel, grid=grid, in_specs=in_specs, out_specs=out_specs, out_shape=out_shape,
        scratch_shapes=list(scratch_shapes), name=name,
        input_output_aliases=input_output_aliases or {},
        compiler_params=pltpu.CompilerParams(dimension_semantics=("arbitrary",) * len(grid)))


def _rmsnorm_kernel(x_ref, g_ref, o_ref):
    x = x_ref[...]
    ms = jnp.mean(x * x, axis=-1, keepdims=True)
    o_ref[...] = ((x * lax.rsqrt(ms + EPS)) * g_ref[...]).astype(o_ref.dtype)


def _rmsnorm(x2d, g_row, out_dtype, rows):
    m, d = x2d.shape
    return _call(
        _rmsnorm_kernel, grid=(m // rows,),
        in_specs=[pl.BlockSpec((rows, d), lambda i: (i, 0)), pl.BlockSpec((1, d), lambda i: (0, 0))],
        out_specs=pl.BlockSpec((rows, d), lambda i: (i, 0)),
        out_shape=jax.ShapeDtypeStruct((m, d), out_dtype), name="rmsnorm")(x2d, g_row)


def _mm_kernel(a_ref, w_ref, o_ref):
    o_ref[...] = _dot(a_ref[...], w_ref[...]).astype(o_ref.dtype)


def _mm(a, w_stack, layer, col0, n, tm, tn, name):
    m, k = a.shape
    c0 = col0 // tn
    return _call(
        _mm_kernel, grid=(m // tm, n // tn),
        in_specs=[pl.BlockSpec((tm, k), lambda i, j: (i, 0)),
                  pl.BlockSpec((None, k, tn), lambda i, j: (layer, 0, j + c0))],
        out_specs=pl.BlockSpec((tm, tn), lambda i, j: (i, j)),
        out_shape=jax.ShapeDtypeStruct((m, n), F32), name=name)(a, w_stack)


def _mm_res_kernel(a_ref, w_ref, r_ref, o_ref):
    o_ref[...] = r_ref[...] + _dot(a_ref[...], w_ref[...])


def _mm_res(a, w_stack, layer, res, tm, tn, name):
    m, k = a.shape
    n = w_stack.shape[-1]
    return _call(
        _mm_res_kernel, grid=(m // tm, n // tn),
        in_specs=[pl.BlockSpec((tm, k), lambda i, j: (i, 0)),
                  pl.BlockSpec((None, k, tn), lambda i, j: (layer, 0, j)),
                  pl.BlockSpec((tm, tn), lambda i, j: (i, j))],
        out_specs=pl.BlockSpec((tm, tn), lambda i, j: (i, j)),
        out_shape=jax.ShapeDtypeStruct((m, n), F32), name=name)(a, w_stack, res)


def _l2norm_heads(x, dk, scale):
    parts = []
    for i in range(x.shape[1] // dk):
        p = x[:, i * dk:(i + 1) * dk]
        p = p * lax.rsqrt(jnp.sum(p * p, axis=-1, keepdims=True) + EPS)
        parts.append(p * scale if scale != 1.0 else p)
    return jnp.concatenate(parts, axis=-1)


def _qkv_finish(c, o_write, n, n_q, n_k, dk):
    if n_q == 0 and n_k == 0:
        o_write(c)
        return

    @pl.when(n < n_q)
    def _():
        o_write(_l2norm_heads(c, dk, dk ** -0.5))

    @pl.when(jnp.logical_and(n >= n_q, n < n_q + n_k))
    def _():
        o_write(_l2norm_heads(c, dk, 1.0))

    @pl.when(n >= n_q + n_k)
    def _():
        o_write(c)


def _qkv_prompt_kernel(h_ref, w_ref, cw_ref, qkv_ref, nb_ref, zs_ref, *, seq, pad, taps, n_q, n_k, dk):
    n = pl.program_id(1)
    z = _dot(h_ref[...], w_ref[...].astype(BF16))
    zs_ref[0:SUBLANES, :] = jnp.zeros((SUBLANES, z.shape[1]), F32)
    zs_ref[SUBLANES:SUBLANES + seq, :] = z
    nb_ref[...] = zs_ref[SUBLANES + seq - (taps - 1):SUBLANES + seq, :]
    acc = z * cw_ref[taps - 1:taps, :]
    for i in range(taps - 1):
        s = taps - 1 - i
        acc = acc + zs_ref[SUBLANES - s:SUBLANES - s + seq, :] * cw_ref[i:i + 1, :]
    c = _silu(acc)
    qkv_ref[0:pad, :] = jnp.zeros((pad, z.shape[1]), qkv_ref.dtype)

    def write(val):
        qkv_ref[pad:pad + seq, :] = val.astype(qkv_ref.dtype)

    _qkv_finish(c, write, n, n_q, n_k, dk)


def _qkv_prompt(h3, w_in, layer, cw, pad, dk, col0, width, n_q, n_k, out_dtype):
    b, seq, d = h3.shape
    taps = cw.shape[0]
    tn = MXU_COLS
    c0 = col0 // tn
    kern = functools.partial(_qkv_prompt_kernel, seq=seq, pad=pad, taps=taps, n_q=n_q, n_k=n_k, dk=dk)
    return _call(
        kern, grid=(b, width // tn),
        in_specs=[pl.BlockSpec((None, seq, d), lambda i, j: (i, 0, 0), pipeline_mode=pl.Buffered(1)),
                  pl.BlockSpec((None, d, tn), lambda i, j: (layer, 0, j + c0)),
                  pl.BlockSpec((taps, tn), lambda i, j: (0, j + c0))],
        out_specs=[pl.BlockSpec((None, seq + pad, tn), lambda i, j: (i, 0, j)),
                   pl.BlockSpec((None, taps - 1, tn), lambda i, j: (i, 0, j))],
        out_shape=[jax.ShapeDtypeStruct((b, seq + pad, width), out_dtype),
                   jax.ShapeDtypeStruct((b, taps - 1, width), F32)],
        scratch_shapes=[pltpu.VMEM((seq + SUBLANES, tn), F32)], name="qkv_prompt")(h3, w_in, cw)


def _qkv_sample_kernel(h_ref, w_ref, cw_ref, st_ref, qkv_ref, z_ref, *, taps, n_q, n_k, dk):
    n = pl.program_id(0)
    z = _dot(h_ref[...], w_ref[...].astype(BF16))
    z_ref[...] = z
    acc = z * cw_ref[taps - 1:taps, :]
    for i in range(taps - 1):
        acc = acc + st_ref[i] * cw_ref[i:i + 1, :]
    c = _silu(acc)

    def write(val):
        qkv_ref[...] = val

    _qkv_finish(c, write, n, n_q, n_k, dk)


def _qkv_sample(h2, w_stack, layer, cw, st_stack, dk, key_w):
    bs, d = h2.shape
    taps, width = cw.shape
    tn = MXU_COLS
    kern = functools.partial(_qkv_sample_kernel, taps=taps, n_q=key_w // tn, n_k=key_w // tn, dk=dk)
    return _call(
        kern, grid=(width // tn,),
        in_specs=[pl.BlockSpec((bs, d), lambda j: (0, 0)),
                  pl.BlockSpec((None, d, tn), lambda j: (layer, 0, j)),
                  pl.BlockSpec((taps, tn), lambda j: (0, j)),
                  pl.BlockSpec((None, taps - 1, bs, tn), lambda j: (layer, 0, 0, j))],
        out_specs=[pl.BlockSpec((bs, tn), lambda j: (0, j)), pl.BlockSpec((bs, tn), lambda j: (0, j))],
        out_shape=[jax.ShapeDtypeStruct((bs, width), F32), jax.ShapeDtypeStruct((bs, width), F32)],
        name="qkv_sample")(h2, w_stack, cw, st_stack)


def _gates_math(ab, alog_ref, dtb_ref):
    g = -jnp.exp(alog_ref[...]) * _softplus(ab + dtb_ref[...])
    return g, _sigmoid(ab)


def _expand_heads(x, first_head, lane_off, heads, dk):
    src = lax.broadcasted_iota(jnp.int32, (LANES, heads * dk), 0)
    dst = lax.broadcasted_iota(jnp.int32, (LANES, heads * dk), 1) // dk + (first_head + lane_off)
    return _dot(x, (src == dst).astype(F32), precision=HIGHEST)


def _gates_prompt_kernel(h_ref, wab_ref, alog_ref, dtb_ref, g_ref, b_ref, gp_ref, bp_ref,
                         *, seq, pad, n_heads, dk, heads_per_step):
    j = pl.program_id(1)

    @pl.when(j == 0)
    def _():
        ab = _dot(h_ref[...], wab_ref[...])
        g, beta = _gates_math(ab, alog_ref, dtb_ref)
        zeros = jnp.zeros((pad, LANES), F32)
        gp_ref[0:pad, :] = zeros
        bp_ref[0:pad, :] = zeros
        gp_ref[pad:pad + seq, :] = g
        bp_ref[pad:pad + seq, :] = beta
        row = lax.broadcasted_iota(jnp.int32, (CHUNK, CHUNK), 0)
        col = lax.broadcasted_iota(jnp.int32, (CHUNK, CHUNK), 1)
        tril = (row >= col).astype(F32)
        for c in range((seq + pad) // CHUNK):
            rows = slice(c * CHUNK, (c + 1) * CHUNK)
            gp_ref[rows, :] = _dot(tril, gp_ref[rows, :], precision=HIGHEST)

    g_ref[...] = _expand_heads(gp_ref[...], j * heads_per_step, 0, heads_per_step, dk)
    b_ref[...] = _expand_heads(bp_ref[...], j * heads_per_step, n_heads, heads_per_step, dk)


def _gates_prompt(h3, wab_stack, layer, alog_row, dtb_row, pad, n_heads, dk):
    b, seq, d = h3.shape
    t = seq + pad
    hps = 4
    kern = functools.partial(_gates_prompt_kernel, seq=seq, pad=pad, n_heads=n_heads, dk=dk, heads_per_step=hps)
    out = jax.ShapeDtypeStruct((b, t, n_heads * dk), F32)
    return _call(
        kern, grid=(b, n_heads // hps),
        in_specs=[pl.BlockSpec((None, seq, d), lambda i, j: (i, 0, 0), pipeline_mode=pl.Buffered(1)),
                  pl.BlockSpec((None, d, LANES), lambda i, j: (layer, 0, 0)),
                  pl.BlockSpec((1, LANES), lambda i, j: (0, 0)),
                  pl.BlockSpec((1, LANES), lambda i, j: (0, 0))],
        out_specs=[pl.BlockSpec((None, t, hps * dk), lambda i, j: (i, 0, j)),
                   pl.BlockSpec((None, t, hps * dk), lambda i, j: (i, 0, j))],
        out_shape=[out, out],
        scratch_shapes=[pltpu.VMEM((t, LANES), F32), pltpu.VMEM((t, LANES), F32)],
        name="gates_prompt")(h3, wab_stack, alog_row, dtb_row)


def _gates_sample_kernel(h_ref, wab_ref, alog_ref, dtb_ref, g_ref, b_ref, *, n_heads, dk):
    ab = _dot(h_ref[...], wab_ref[...])
    g, beta = _gates_math(ab, alog_ref, dtb_ref)
    g_ref[...] = _expand_heads(g, 0, 0, n_heads, dk)
    b_ref[...] = _expand_heads(beta, 0, n_heads, n_heads, dk)


def _gates_sample(h2, wab_stack, layer, alog_row, dtb_row, n_heads, dk):
    bs, d = h2.shape
    out = jax.ShapeDtypeStruct((bs, n_heads * dk), F32)
    return _call(
        functools.partial(_gates_sample_kernel, n_heads=n_heads, dk=dk), grid=(1,),
        in_specs=[pl.BlockSpec((bs, d), lambda i: (0, 0)),
                  pl.BlockSpec((None, d, LANES), lambda i: (layer, 0, 0)),
                  pl.BlockSpec((1, LANES), lambda i: (0, 0)),
                  pl.BlockSpec((1, LANES), lambda i: (0, 0))],
        out_specs=[pl.BlockSpec((bs, n_heads * dk), lambda i: (0, 0)),
                   pl.BlockSpec((bs, n_heads * dk), lambda i: (0, 0))],
        out_shape=[out, out], name="gates_sample")(h2, wab_stack, alog_row, dtb_row)


def _mixb_prompt_kernel(h_ref, wb_ref, wc_ref, wh_ref, cw_ref, y_ref, nb_ref, ps_ref, *, seq, taps):
    h = h_ref[...]
    pre = _dot(h, wc_ref[...]) * _dot(h, wh_ref[...])
    ps_ref[0:SUBLANES, :] = jnp.zeros((SUBLANES, pre.shape[1]), F32)
    ps_ref[SUBLANES:SUBLANES + seq, :] = pre
    nb_ref[...] = ps_ref[SUBLANES + seq - (taps - 1):SUBLANES + seq, :]
    acc = pre * cw_ref[taps - 1:taps, :]
    for i in range(taps - 1):
        s = taps - 1 - i
        acc = acc + ps_ref[SUBLANES - s:SUBLANES - s + seq, :] * cw_ref[i:i + 1, :]
    y_ref[...] = (_dot(h, wb_ref[...]) * acc).astype(y_ref.dtype)


def _mixb_prompt(h3, w_stack, cols, layer, cw):
    b, seq, d = h3.shape
    taps, width = cw.shape
    tn = MXU_COLS

    def wspec(col0):
        return pl.BlockSpec((None, d, tn), lambda i, j: (layer, 0, j + col0 // tn))

    return _call(
        functools.partial(_mixb_prompt_kernel, seq=seq, taps=taps), grid=(b, width // tn),
        in_specs=[pl.BlockSpec((None, seq, d), lambda i, j: (i, 0, 0), pipeline_mode=pl.Buffered(1)),
                  wspec(cols[0]), wspec(cols[1]), wspec(cols[2]), pl.BlockSpec((taps, tn), lambda i, j: (0, j))],
        out_specs=[pl.BlockSpec((None, seq, tn), lambda i, j: (i, 0, j)),
                   pl.BlockSpec((None, taps - 1, tn), lambda i, j: (i, 0, j))],
        out_shape=[jax.ShapeDtypeStruct((b, seq, width), BF16),
                   jax.ShapeDtypeStruct((b, taps - 1, width), F32)],
        scratch_shapes=[pltpu.VMEM((seq + SUBLANES, tn), F32)],
        name="mixb_prompt")(h3, w_stack, w_stack, w_stack, cw)


def _mixb_sample_kernel(h_ref, wb_ref, wc_ref, wh_ref, cw_ref, st_ref, y_ref, pre_ref, *, taps):
    h = h_ref[...]
    pre = _dot(h, wc_ref[...]) * _dot(h, wh_ref[...])
    pre_ref[...] = pre
    acc = pre * cw_ref[taps - 1:taps, :]
    for i in range(taps - 1):
        acc = acc + st_ref[i] * cw_ref[i:i + 1, :]
    y_ref[...] = (_dot(h, wb_ref[...]) * acc).astype(y_ref.dtype)


def _mixb_sample(h2, w_stack, cols, layer, cw, st_stack):
    bs, d = h2.shape
    taps, width = cw.shape
    tn = MXU_COLS

    def wspec(col0):
        return pl.BlockSpec((None, d, tn), lambda j: (layer, 0, j + col0 // tn))

    return _call(
        functools.partial(_mixb_sample_kernel, taps=taps), grid=(width // tn,),
        in_specs=[pl.BlockSpec((bs, d), lambda j: (0, 0)), wspec(cols[0]), wspec(cols[1]), wspec(cols[2]),
                  pl.BlockSpec((taps, tn), lambda j: (0, j)),
                  pl.BlockSpec((None, taps - 1, bs, tn), lambda j: (layer, 0, 0, j))],
        out_specs=[pl.BlockSpec((bs, tn), lambda j: (0, j)), pl.BlockSpec((bs, tn), lambda j: (0, j))],
        out_shape=[jax.ShapeDtypeStruct((bs, width), BF16), jax.ShapeDtypeStruct((bs, width), F32)],
        name="mixb_sample")(h2, w_stack, w_stack, w_stack, cw, st_stack)


def _bmm(a, b):
    return jnp.einsum("cij,cjk->cik", a.astype(BF16), b.astype(BF16), preferred_element_type=F32)


def _bmm_nt(a, b):
    return jnp.einsum("cid,cjd->cij", a.astype(BF16), b.astype(BF16), preferred_element_type=F32)


def _unit_lower_inverse(lm, row, col, eye):
    blk16 = (row // 16) == (col // 16)
    blk32 = (row // 32) == (col // 32)
    ld = jnp.where(blk16, lm, 0.0)
    lo1 = jnp.where(jnp.logical_and(blk32, jnp.logical_not(blk16)), lm, 0.0)
    lo2 = jnp.where(blk32, 0.0, lm)
    x = eye.astype(F32) - ld
    m = _bmm(ld, ld)
    x = x + _bmm(x, m)
    m = _bmm(m, m)
    x = x + _bmm(x, m)
    m = _bmm(m, m)
    x = x + _bmm(x, m)
    x = x - _bmm(x, _bmm(lo1, x))
    x = x - _bmm(x, _bmm(lo2, x))
    return x


HEADS_PER_STEP = 2


def _delta_prompt_kernel(q_ref, k_ref, v_ref, g_ref, b_ref, go_ref, og_ref, o_ref, s_ref,
                         w_s, u_s, kt_s, qk_s, oo_s, *, seq, pad, dk):
    t = seq + pad
    nc = t // CHUNK
    row = lax.broadcasted_iota(jnp.int32, (CHUNK, CHUNK), 0)
    col = lax.broadcasted_iota(jnp.int32, (CHUNK, CHUNK), 1)
    eye = row == col
    incl = row >= col
    for hh in range(HEADS_PER_STEP):
        lanes = slice(hh * dk, (hh + 1) * dk)
        q3 = q_ref[:, lanes].reshape(nc, CHUNK, dk)
        k3 = k_ref[:, lanes].reshape(nc, CHUNK, dk)
        v3 = v_ref[:, lanes].reshape(nc, CHUNK, dk)
        g3 = g_ref[:, lanes].reshape(nc, CHUNK, dk)
        b3 = b_ref[:, lanes].reshape(nc, CHUNK, dk)
        kf = k3.astype(F32)
        gi = g3[:, :, :CHUNK]
        gj = jnp.sum(jnp.where(eye, gi, 0.0), axis=1, keepdims=True)
        decay = jnp.where(incl, jnp.exp(jnp.where(incl, gi - gj, 0.0)), 0.0)
        lm = jnp.where(row > col, b3[:, :, :CHUNK] * _bmm_nt(k3, k3) * decay, 0.0)
        tinv = _unit_lower_inverse(lm, row, col, eye)
        u_s[hh] = _bmm(tinv, b3 * v3).reshape(t, dk)
        w_s[hh] = _bmm(tinv, b3 * jnp.exp(g3) * kf).reshape(t, dk).astype(BF16)
        qk_s[hh] = (_bmm_nt(q3, k3) * decay).reshape(t, CHUNK).astype(BF16)
        kt_s[hh] = (kf * jnp.exp(g3[:, CHUNK - 1:CHUNK, :] - g3)).reshape(t, dk).astype(BF16)

    def body(c, states):
        r0 = pl.multiple_of(c * CHUNK, CHUNK)
        rows = pl.ds(r0, CHUNK)
        new_states = []
        for hh in range(HEADS_PER_STEP):
            lanes = slice(hh * dk, (hh + 1) * dk)
            s = states[hh]
            sb = s.astype(BF16)
            u = u_s[hh, rows, :] - _dot(w_s[hh, rows, :], sb)
            ub = u.astype(BF16)
            oo_s[hh, rows, :] = (jnp.exp(g_ref[rows, lanes]) * _dot(q_ref[rows, lanes], sb)
                                 + _dot(qk_s[hh, rows, :], ub))
            g_tail = g_ref[pl.ds(pl.multiple_of(r0 + CHUNK - SUBLANES, SUBLANES), SUBLANES), lanes]
            s_scale = jnp.exp(g_tail[SUBLANES - 1:SUBLANES, :])
            new_states.append(s_scale * s + lax.dot_general(kt_s[hh, rows, :], ub, (((0,), (0,)), ((), ())),
                                                            preferred_element_type=F32))
        return tuple(new_states)

    init = tuple(jnp.zeros((dk, dk), F32) for _ in range(HEADS_PER_STEP))
    states = lax.fori_loop(0, nc, body, init)
    for hh in range(HEADS_PER_STEP):
        lanes = slice(hh * dk, (hh + 1) * dk)
        s_ref[hh] = states[hh]
        o = oo_s[hh, pad:pad + seq, :]
        o = (o * lax.rsqrt(jnp.mean(o * o, axis=-1, keepdims=True) + EPS)) * og_ref[...]
        o_ref[:, lanes] = (o * _silu(go_ref[:, lanes])).astype(o_ref.dtype)


def _delta_prompt(qk, v, gcb, bcb, gout3, og_row, pad, n_heads, dk):
    b, t, _ = v.shape
    seq = t - pad
    hps = HEADS_PER_STEP
    wd = hps * dk

    def col_spec(rows, off):
        return pl.BlockSpec((None, rows, wd), lambda i, j: (i, 0, j + off))

    kern = functools.partial(_delta_prompt_kernel, seq=seq, pad=pad, dk=dk)
    return _call(
        kern, grid=(b, n_heads // hps),
        in_specs=[col_spec(t, 0), col_spec(t, n_heads // hps), col_spec(t, 0),
                  col_spec(t, 0), col_spec(t, 0), col_spec(seq, 0),
                  pl.BlockSpec((1, dk), lambda i, j: (0, 0))],
        out_specs=[col_spec(seq, 0), pl.BlockSpec((None, hps, dk, dk), lambda i, j: (i, j, 0, 0))],
        out_shape=[jax.ShapeDtypeStruct((b, seq, n_heads * dk), BF16),
                   jax.ShapeDtypeStruct((b, n_heads, dk, dk), F32)],
        scratch_shapes=[pltpu.VMEM((hps, t, dk), BF16), pltpu.VMEM((hps, t, dk), F32),
                        pltpu.VMEM((hps, t, dk), BF16), pltpu.VMEM((hps, t, CHUNK), BF16),
                        pltpu.VMEM((hps, t, dk), F32)],
        name="delta_prompt")(qk, qk, v, gcb, bcb, gout3, og_row)


def _delta_sample_kernel(q_ref, k_ref, v_ref, g_ref, b_ref, go_ref, og_ref, s_ref, *rest, n_heads, dk):
    o_ref, sn_ref = rest[-2:]
    q = q_ref[...]
    k = k_ref[...]
    v = v_ref[...]
    kq = jnp.concatenate([k, q, jnp.zeros((LANES - 2 * n_heads, dk), F32)], axis=0)
    kq_t = kq.T
    qk = jnp.sum(q * k, axis=-1, keepdims=True)
    outs = []
    for h in range(n_heads):
        s = s_ref[h]
        a = jnp.exp(g_ref[:, h * dk:(h + 1) * dk])
        beta = b_ref[:, h * dk:(h + 1) * dk]
        kcol = kq_t[:, h:h + 1]
        qcol = kq_t[:, n_heads + h:n_heads + h + 1]
        ks = jnp.sum(kcol * s, axis=0, keepdims=True)
        qs = jnp.sum(qcol * s, axis=0, keepdims=True)
        u = beta * (v[h:h + 1, :] - a * ks)
        outs.append(a * qs + qk[h:h + 1, :] * u)
        sn_ref[h] = a * s + kcol * u
    o = jnp.concatenate(outs, axis=0)
    o = (o * lax.rsqrt(jnp.mean(o * o, axis=-1, keepdims=True) + EPS)) * og_ref[...]
    o_ref[...] = (o * _silu(go_ref[...])).astype(o_ref.dtype)


def _delta_sample(q3, k3, v3, gcb, bcb, gout3, og_row, state, layer, prev_out):
    bs, n_heads, dk = q3.shape
    depth = state.shape[0]
    hspec = pl.BlockSpec((None, n_heads, dk), lambda i: (i, 0, 0))
    rspec = pl.BlockSpec((None, 1, n_heads * dk), lambda i: (i, 0, 0))
    sspec = pl.BlockSpec((None, None, n_heads, dk, dk), lambda i: (layer, i, 0, 0, 0))
    in_specs = [hspec, hspec, hspec, rspec, rspec, hspec, pl.BlockSpec((1, dk), lambda i: (0, 0)), sspec]
    args = [q3, k3, v3, gcb.reshape(bs, 1, -1), bcb.reshape(bs, 1, -1), gout3, og_row, state]
    aliases = {}
    if prev_out is not None:
        aliases = {len(args): 1}
        in_specs.append(pl.BlockSpec(memory_space=pl.ANY))
        args.append(prev_out)
    return _call(
        functools.partial(_delta_sample_kernel, n_heads=n_heads, dk=dk), grid=(bs,),
        in_specs=in_specs, out_specs=[hspec, sspec],
        out_shape=[jax.ShapeDtypeStruct((bs, n_heads, dk), BF16),
                   jax.ShapeDtypeStruct((depth, bs, n_heads, dk, dk), F32)],
        input_output_aliases=aliases, name="delta_sample")(*args)


def _merge_kernel(o_ref, y_ref, h_ref, wa_ref, wb_ref, wza_ref, wzb_ref, m_ref):
    h = h_ref[...]
    za = _sigmoid(_dot(h, wza_ref[...]))
    zb = _sigmoid(_dot(h, wzb_ref[...]))
    m_ref[...] = (za * _dot(o_ref[...], wa_ref[...]) + zb * _dot(y_ref[...], wb_ref[...])).astype(m_ref.dtype)


def _merge(o2, y2, h2, wa_stack, wb_stack, wz_stack, col_za, col_zb, layer, tm):
    m, d = h2.shape
    ka = o2.shape[1]
    kb = y2.shape[1]
    n = wa_stack.shape[-1]
    tn = MXU_COLS

    def wspec(kk, col0=0):
        return pl.BlockSpec((None, kk, tn), lambda i, j: (layer, 0, j + col0 // tn))

    return _call(
        _merge_kernel, grid=(m // tm, n // tn),
        in_specs=[pl.BlockSpec((tm, ka), lambda i, j: (i, 0)), pl.BlockSpec((tm, kb), lambda i, j: (i, 0)),
                  pl.BlockSpec((tm, d), lambda i, j: (i, 0)), wspec(ka), wspec(kb),
                  wspec(d, col_za), wspec(d, col_zb)],
        out_specs=pl.BlockSpec((tm, tn), lambda i, j: (i, j)),
        out_shape=jax.ShapeDtypeStruct((m, n), BF16), name="merge")(
            o2, y2, h2, wa_stack, wb_stack, wz_stack, wz_stack)


def _ffn_prompt_kernel(h_ref, wg_ref, wv_ref, cw_ref, cb_ref, a_ref, nb_ref, gs_ref, *, seq, taps):
    h = h_ref[...]
    gate = _dot(h, wg_ref[...])
    gs_ref[0:SUBLANES, :] = jnp.zeros((SUBLANES, gate.shape[1]), F32)
    gs_ref[SUBLANES:SUBLANES + seq, :] = gate
    nb_ref[...] = gs_ref[SUBLANES + seq - (taps - 1):SUBLANES + seq, :]
    acc = gate * cw_ref[taps - 1:taps, :]
    for i in range(taps - 1):
        s = taps - 1 - i
        acc = acc + gs_ref[SUBLANES - s:SUBLANES - s + seq, :] * cw_ref[i:i + 1, :]
    a_ref[...] = (_silu(acc + cb_ref[...]) * _dot(h, wv_ref[...])).astype(a_ref.dtype)


def _ffn_prompt(h3, wup_stack, layer, cw, cb_row):
    b, seq, d = h3.shape
    dff = cw.shape[1]
    taps = cw.shape[0]
    tn = MXU_COLS
    nt = dff // tn
    return _call(
        functools.partial(_ffn_prompt_kernel, seq=seq, taps=taps), grid=(b, nt),
        in_specs=[pl.BlockSpec((None, seq, d), lambda i, j: (i, 0, 0), pipeline_mode=pl.Buffered(1)),
                  pl.BlockSpec((None, d, tn), lambda i, j: (layer, 0, j)),
                  pl.BlockSpec((None, d, tn), lambda i, j: (layer, 0, j + nt)),
                  pl.BlockSpec((taps, tn), lambda i, j: (0, j)),
                  pl.BlockSpec((1, tn), lambda i, j: (0, j))],
        out_specs=[pl.BlockSpec((None, seq, tn), lambda i, j: (i, 0, j)),
                   pl.BlockSpec((None, taps - 1, tn), lambda i, j: (i, 0, j))],
        out_shape=[jax.ShapeDtypeStruct((b, seq, dff), BF16),
                   jax.ShapeDtypeStruct((b, taps - 1, dff), F32)],
        scratch_shapes=[pltpu.VMEM((seq + SUBLANES, tn), F32)],
        name="ffn_prompt")(h3, wup_stack, wup_stack, cw, cb_row)


def _ffn_sample_kernel(h_ref, wg_ref, wv_ref, cw_ref, cb_ref, st_ref, a_ref, g_ref, *, taps):
    h = h_ref[...]
    gate = _dot(h, wg_ref[...])
    g_ref[...] = gate
    acc = gate * cw_ref[taps - 1:taps, :]
    for i in range(taps - 1):
        acc = acc + st_ref[i] * cw_ref[i:i + 1, :]
    a_ref[...] = (_silu(acc + cb_ref[...]) * _dot(h, wv_ref[...])).astype(a_ref.dtype)


def _ffn_sample(h2, wup_stack, layer, cw, cb_row, st_stack):
    bs, d = h2.shape
    dff = cw.shape[1]
    taps = cw.shape[0]
    tn = MXU_COLS
    nt = dff // tn
    return _call(
        functools.partial(_ffn_sample_kernel, taps=taps), grid=(nt,),
        in_specs=[pl.BlockSpec((bs, d), lambda j: (0, 0)),
                  pl.BlockSpec((None, d, tn), lambda j: (layer, 0, j)),
                  pl.BlockSpec((None, d, tn), lambda j: (layer, 0, j + nt)),
                  pl.BlockSpec((taps, tn), lambda j: (0, j)),
                  pl.BlockSpec((1, tn), lambda j: (0, j)),
                  pl.BlockSpec((None, taps - 1, bs, tn), lambda j: (layer, 0, 0, j))],
        out_specs=[pl.BlockSpec((bs, tn), lambda j: (0, j)), pl.BlockSpec((bs, tn), lambda j: (0, j))],
        out_shape=[jax.ShapeDtypeStruct((bs, dff), BF16), jax.ShapeDtypeStruct((bs, dff), F32)],
        name="ffn_sample")(h2, wup_stack, wup_stack, cw, cb_row, st_stack)


def _repack_kernel(a_ref, b_ref, o_ref, *, shift):
    o_ref[...] = jnp.concatenate([a_ref[:, shift:], b_ref[:, :shift]], axis=1).astype(o_ref.dtype)


def _repack_bf16(w, col0, width):
    depth, k, _ = w.shape
    tn = MXU_COLS
    shift = col0 % LANES
    base = col0 - shift
    assert shift and base % tn == 0 and width % tn == 0
    a0 = base // tn
    b0 = (base + tn) // LANES
    lane_blocks = tn // LANES
    return _call(
        functools.partial(_repack_kernel, shift=shift), grid=(depth, width // tn),
        in_specs=[pl.BlockSpec((None, k, tn), lambda l, c: (l, 0, a0 + c)),
                  pl.BlockSpec((None, k, LANES), lambda l, c: (l, 0, b0 + lane_blocks * c))],
        out_specs=pl.BlockSpec((None, k, tn), lambda l, c: (l, 0, c)),
        out_shape=jax.ShapeDtypeStruct((depth, k, width), BF16), name="repack")(w, w)


def _pad_lanes(x2d):
    return jnp.pad(x2d, ((0, 0), (0, LANES - x2d.shape[1])))


def _roll_state(state_l, new_row):
    return jnp.concatenate([state_l[:, 1:], new_row[:, None]], axis=1)


def kernel(x_prompt, x_sample, state_delta, state_conv_qkv, state_conv_mix, state_conv_ffn, meta_tokens,
           norm1_g, w_in, conv_qkv_w, a_log, dt_bias, onorm_g, conv_mix_w, w_up_a, w_up_b, w_o, norm2_g,
           w_ffn_up, conv_ffn_w, conv_ffn_b, w_ffn_down, final_g):
    b, seq0, d = x_prompt.shape
    bs = x_sample.shape[0]
    depth, _, n_heads, dk, dv = state_delta.shape
    assert dk == dv == LANES and x_sample.shape[1] == 1
    n_meta = meta_tokens.shape[0]
    seq = seq0 + n_meta
    pad = (-seq) % CHUNK
    key_w = n_heads * dk
    val_w = n_heads * dv
    qkv_w = state_conv_qkv.shape[-1]
    w_b = state_conv_mix.shape[-1]
    d_ff = state_conv_ffn.shape[-1]
    assert 2 * n_heads <= LANES

    o_ab = qkv_w
    o_go = o_ab + 2 * n_heads
    o_bg = o_go + val_w
    o_cg = o_bg + w_b
    o_hb = o_cg + w_b
    o_za = o_hb + w_b
    o_zb = o_za + d
    assert o_zb + d == w_in.shape[-1]

    w_rest = _repack_bf16(w_in, o_go, w_in.shape[-1] - o_go)
    c_go, c_bg, c_cg, c_hb, c_za, c_zb = (o - o_go for o in (o_go, o_bg, o_cg, o_hb, o_za, o_zb))
    w_ab = jnp.pad(w_in[:, :, o_ab:o_go], ((0, 0), (0, 0), (0, LANES - 2 * n_heads))).astype(BF16)
    wua, wub, wo, wfu, wfd = (w.astype(BF16) for w in (w_up_a, w_up_b, w_o, w_ffn_up, w_ffn_down))
    st_qkv = jnp.swapaxes(state_conv_qkv, 1, 2)
    st_mix = jnp.swapaxes(state_conv_mix, 1, 2)
    st_ffn = jnp.swapaxes(state_conv_ffn, 1, 2)

    xp = jnp.concatenate([jnp.broadcast_to(meta_tokens[None], (b, n_meta, d)), x_prompt], axis=1)
    xp = xp.reshape(b * seq, d)
    xs = x_sample.reshape(bs, d)
    m = b * seq
    tm = seq // 3
    rows_norm = 192
    assert seq % 3 == 0 and tm % 16 == 0 and m % rows_norm == 0
    tn_o = 512
    tn_f = MXU_COLS

    sp, qp, qs, mp, ms, fp, fs = [], [], [], [], [], [], []
    s_sample = None
    for l in range(depth):
        alog_row = _pad_lanes(a_log[l][None])
        dtb_row = _pad_lanes(dt_bias[l][None])
        og_row = onorm_g[l][None]
        cb_row = conv_ffn_b[l][None]

        h = _rmsnorm(xp, norm1_g[l][None], BF16, rows_norm)
        h3 = h.reshape(b, seq, d)
        qk, nb_qk = _qkv_prompt(h3, w_in, l, conv_qkv_w[l], pad, dk, 0, 2 * key_w,
                                key_w // MXU_COLS, key_w // MXU_COLS, BF16)
        v, nb_v = _qkv_prompt(h3, w_in, l, conv_qkv_w[l], pad, dk, 2 * key_w, val_w, 0, 0, F32)
        nb_q = jnp.concatenate([nb_qk, nb_v], axis=-1)
        gcb, bcb = _gates_prompt(h3, w_ab, l, alog_row, dtb_row, pad, n_heads, dk)
        gout = _mm(h, w_rest, l, c_go, val_w, tm, 1024, "gout")
        y_b, nb_m = _mixb_prompt(h3, w_rest, (c_bg, c_cg, c_hb), l, conv_mix_w[l])
        o, s_new = _delta_prompt(qk, v, gcb, bcb, gout.reshape(b, seq, val_w), og_row, pad, n_heads, dk)
        merged = _merge(o.reshape(m, val_w), y_b.reshape(m, w_b), h, wua, wub, w_rest, c_za, c_zb, l, tm)
        xp = _mm_res(merged, wo, l, xp, tm, tn_o, "out_proj")
        h2 = _rmsnorm(xp, norm2_g[l][None], BF16, rows_norm)
        act, nb_f = _ffn_prompt(h2.reshape(b, seq, d), wfu, l, conv_ffn_w[l], cb_row)
        xp = _mm_res(act.reshape(m, d_ff), wfd, l, xp, tm, tn_f, "ffn_down")
        sp.append(s_new); qp.append(nb_q); mp.append(nb_m); fp.append(nb_f)

        hs = _rmsnorm(xs, norm1_g[l][None], BF16, bs)
        qkv_s, z_new = _qkv_sample(hs, w_in, l, conv_qkv_w[l], st_qkv, dk, key_w)
        gcb_s, bcb_s = _gates_sample(hs, w_ab, l, alog_row, dtb_row, n_heads, dk)
        gout_s = _mm(hs, w_rest, l, c_go, val_w, bs, 1024, "gout")
        yb_s, pre_new = _mixb_sample(hs, w_rest, (c_bg, c_cg, c_hb), l, conv_mix_w[l], st_mix)
        q3 = qkv_s[:, :key_w].reshape(bs, n_heads, dk)
        k3 = qkv_s[:, key_w:2 * key_w].reshape(bs, n_heads, dk)
        v3 = qkv_s[:, 2 * key_w:].reshape(bs, n_heads, dv)
        o_s, s_sample = _delta_sample(q3, k3, v3, gcb_s, bcb_s, gout_s.reshape(bs, n_heads, dv), og_row,
                                      state_delta, l, s_sample)
        merged_s = _merge(o_s.reshape(bs, val_w), yb_s, hs, wua, wub, w_rest, c_za, c_zb, l, bs)
        xs = _mm_res(merged_s, wo, l, xs, bs, tn_o, "out_proj")
        h2s = _rmsnorm(xs, norm2_g[l][None], BF16, bs)
        act_s, gate_new = _ffn_sample(h2s, wfu, l, conv_ffn_w[l], cb_row, st_ffn)
        xs = _mm_res(act_s, wfd, l, xs, bs, tn_f, "ffn_down")
        qs.append(_roll_state(state_conv_qkv[l], z_new))
        ms.append(_roll_state(state_conv_mix[l], pre_new))
        fs.append(_roll_state(state_conv_ffn[l], gate_new))

    y_prompt = _rmsnorm(xp, final_g[None], F32, rows_norm).reshape(b, seq, d)[:, n_meta:]
    y_sample = _rmsnorm(xs, final_g[None], F32, bs).reshape(bs, 1, d)
    return (y_prompt, y_sample, jnp.stack(sp), s_sample, jnp.stack(qp), jnp.stack(qs),
            jnp.stack(mp), jnp.stack(ms), jnp.stack(fp), jnp.stack(fs))
```

```python
import functools

import jax
import jax.numpy as jnp
from jax import lax
from jax.experimental import pallas as pl
from jax.experimental.pallas import tpu as pltpu

F32 = jnp.float32
BF16 = jnp.bfloat16
EPS = 1e-6
HIGHEST = lax.Precision.HIGHEST
CHUNK = 64
LANES = 128
SUBLANES = 8
MXU_COLS = 256


def _sigmoid(x):
    return 1.0 / (1.0 + jnp.exp(-x))


def _silu(x):
    return x * _sigmoid(x)


def _softplus(x):
    return jnp.maximum(x, 0.0) + jnp.log1p(jnp.exp(-jnp.abs(x)))


def _dot(a, b, precision=None):
    return jnp.dot(a, b, preferred_element_type=F32, precision=precision)


def _dot_nt(a, wt):
    return lax.dot_general(a, wt.astype(BF16), (((1,), (1,)), ((), ())), preferred_element_type=F32)


def _call(kernel, *, grid, in_specs, out_specs, out_shape, scratch_shapes=(), name=None,
          input_output_aliases=None):
    return pl.pallas_call(
        kernel, grid=grid, in_specs=in_specs, out_specs=out_specs, out_shape=out_shape,
        scratch_shapes=list(scratch_shapes), name=name,
        input_output_aliases=input_output_aliases or {},
        compiler_params=pltpu.CompilerParams(dimension_semantics=("arbitrary",) * len(grid)))


def _rmsnorm_kernel(x_ref, g_ref, o_ref):
    x = x_ref[...]
    ms = jnp.mean(x * x, axis=-1, keepdims=True)
    o_ref[...] = ((x * lax.rsqrt(ms + EPS)) * g_ref[...]).astype(o_ref.dtype)


def _rmsnorm(x2d, g_row, out_dtype, rows):
    m, d = x2d.shape
    return _call(
        _rmsnorm_kernel, grid=(m // rows,),
        in_specs=[pl.BlockSpec((rows, d), lambda i: (i, 0)), pl.BlockSpec((1, d), lambda i: (0, 0))],
        out_specs=pl.BlockSpec((rows, d), lambda i: (i, 0)),
        out_shape=jax.ShapeDtypeStruct((m, d), out_dtype), name="rmsnorm")(x2d, g_row)


def _mm_kernel(a_ref, w_ref, o_ref):
    o_ref[...] = _dot_nt(a_ref[...], w_ref[...]).astype(o_ref.dtype)


def _mm(a, wt_stack, layer, row0, n, tm, tn, name):
    m, k = a.shape
    r0 = row0 // tn
    return _call(
        _mm_kernel, grid=(m // tm, n // tn),
        in_specs=[pl.BlockSpec((tm, k), lambda i, j: (i, 0)),
                  pl.BlockSpec((None, tn, k), lambda i, j: (layer, j + r0, 0))],
        out_specs=pl.BlockSpec((tm, tn), lambda i, j: (i, j)),
        out_shape=jax.ShapeDtypeStruct((m, n), F32), name=name)(a, wt_stack)


def _mm_res_kernel(a_ref, w_ref, r_ref, o_ref):
    o_ref[...] = r_ref[...] + _dot(a_ref[...], w_ref[...])


def _mm_res(a, w_stack, layer, res, tm, tn, name):
    m, k = a.shape
    n = w_stack.shape[-1]
    return _call(
        _mm_res_kernel, grid=(m // tm, n // tn),
        in_specs=[pl.BlockSpec((tm, k), lambda i, j: (i, 0)),
                  pl.BlockSpec((None, k, tn), lambda i, j: (layer, 0, j)),
                  pl.BlockSpec((tm, tn), lambda i, j: (i, j))],
        out_specs=pl.BlockSpec((tm, tn), lambda i, j: (i, j)),
        out_shape=jax.ShapeDtypeStruct((m, n), F32), name=name)(a, w_stack, res)


def _l2norm_heads(x, dk, scale):
    parts = []
    for i in range(x.shape[1] // dk):
        p = x[:, i * dk:(i + 1) * dk]
        p = p * lax.rsqrt(jnp.sum(p * p, axis=-1, keepdims=True) + EPS)
        parts.append(p * scale if scale != 1.0 else p)
    return jnp.concatenate(parts, axis=-1)


def _qkv_finish(c, o_write, n, n_q, n_k, dk):
    if n_q == 0 and n_k == 0:
        o_write(c)
        return

    @pl.when(n < n_q)
    def _():
        o_write(_l2norm_heads(c, dk, dk ** -0.5))

    @pl.when(jnp.logical_and(n >= n_q, n < n_q + n_k))
    def _():
        o_write(_l2norm_heads(c, dk, 1.0))

    @pl.when(n >= n_q + n_k)
    def _():
        o_write(c)


def _qkv_prompt_kernel(h_ref, w_ref, cw_ref, qkv_ref, nb_ref, zs_ref, *, seq, pad, taps, n_q, n_k, dk):
    n = pl.program_id(1)
    z = _dot_nt(h_ref[...], w_ref[...])
    zs_ref[0:SUBLANES, :] = jnp.zeros((SUBLANES, z.shape[1]), F32)
    zs_ref[SUBLANES:SUBLANES + seq, :] = z
    nb_ref[...] = zs_ref[SUBLANES + seq - (taps - 1):SUBLANES + seq, :]
    acc = z * cw_ref[taps - 1:taps, :]
    for i in range(taps - 1):
        s = taps - 1 - i
        acc = acc + zs_ref[SUBLANES - s:SUBLANES - s + seq, :] * cw_ref[i:i + 1, :]
    c = _silu(acc)
    qkv_ref[0:pad, :] = jnp.zeros((pad, z.shape[1]), qkv_ref.dtype)

    def write(val):
        qkv_ref[pad:pad + seq, :] = val.astype(qkv_ref.dtype)

    _qkv_finish(c, write, n, n_q, n_k, dk)


def _qkv_prompt(h3, w_t, layer, cw, pad, dk, col0, width, n_q, n_k, out_dtype):
    b, seq, d = h3.shape
    taps = cw.shape[0]
    tn = MXU_COLS
    c0 = col0 // tn
    kern = functools.partial(_qkv_prompt_kernel, seq=seq, pad=pad, taps=taps, n_q=n_q, n_k=n_k, dk=dk)
    return _call(
        kern, grid=(b, width // tn),
        in_specs=[pl.BlockSpec((None, seq, d), lambda i, j: (i, 0, 0), pipeline_mode=pl.Buffered(1)),
                  pl.BlockSpec((None, tn, d), lambda i, j: (layer, j + c0, 0)),
                  pl.BlockSpec((taps, tn), lambda i, j: (0, j + c0))],
        out_specs=[pl.BlockSpec((None, seq + pad, tn), lambda i, j: (i, 0, j)),
                   pl.BlockSpec((None, taps - 1, tn), lambda i, j: (i, 0, j))],
        out_shape=[jax.ShapeDtypeStruct((b, seq + pad, width), out_dtype),
                   jax.ShapeDtypeStruct((b, taps - 1, width), F32)],
        scratch_shapes=[pltpu.VMEM((seq + SUBLANES, tn), F32)], name="qkv_prompt")(h3, w_t, cw)


def _qkv_sample_kernel(h_ref, w_ref, cw_ref, st_ref, qkv_ref, z_ref, *, taps, n_q, n_k, dk):
    n = pl.program_id(0)
    z = _dot_nt(h_ref[...], w_ref[...])
    z_ref[...] = z
    acc = z * cw_ref[taps - 1:taps, :]
    for i in range(taps - 1):
        acc = acc + st_ref[i] * cw_ref[i:i + 1, :]
    c = _silu(acc)

    def write(val):
        qkv_ref[...] = val

    _qkv_finish(c, write, n, n_q, n_k, dk)


def _qkv_sample(h2, w_stack, layer, cw, st_stack, dk, key_w):
    bs, d = h2.shape
    taps, width = cw.shape
    tn = MXU_COLS
    kern = functools.partial(_qkv_sample_kernel, taps=taps, n_q=key_w // tn, n_k=key_w // tn, dk=dk)
    return _call(
        kern, grid=(width // tn,),
        in_specs=[pl.BlockSpec((bs, d), lambda j: (0, 0)),
                  pl.BlockSpec((None, tn, d), lambda j: (layer, j, 0)),
                  pl.BlockSpec((taps, tn), lambda j: (0, j)),
                  pl.BlockSpec((None, taps - 1, bs, tn), lambda j: (layer, 0, 0, j))],
        out_specs=[pl.BlockSpec((bs, tn), lambda j: (0, j)), pl.BlockSpec((bs, tn), lambda j: (0, j))],
        out_shape=[jax.ShapeDtypeStruct((bs, width), F32), jax.ShapeDtypeStruct((bs, width), F32)],
        name="qkv_sample")(h2, w_stack, cw, st_stack)


def _gates_math(ab, alog_ref, dtb_ref):
    g = -jnp.exp(alog_ref[...]) * _softplus(ab + dtb_ref[...])
    return g, _sigmoid(ab)


def _expand_heads(x, first_head, lane_off, heads, dk):
    src = lax.broadcasted_iota(jnp.int32, (LANES, heads * dk), 0)
    dst = lax.broadcasted_iota(jnp.int32, (LANES, heads * dk), 1) // dk + (first_head + lane_off)
    return _dot(x, (src == dst).astype(F32), precision=HIGHEST)


def _gates_prompt_kernel(h_ref, wab_ref, alog_ref, dtb_ref, g_ref, b_ref, gp_ref, bp_ref,
                         *, seq, pad, n_heads, dk, heads_per_step):
    j = pl.program_id(1)

    @pl.when(j == 0)
    def _():
        ab = _dot_nt(h_ref[...], wab_ref[...])
        g, beta = _gates_math(ab, alog_ref, dtb_ref)
        zeros = jnp.zeros((pad, LANES), F32)
        gp_ref[0:pad, :] = zeros
        bp_ref[0:pad, :] = zeros
        gp_ref[pad:pad + seq, :] = g
        bp_ref[pad:pad + seq, :] = beta
        row = lax.broadcasted_iota(jnp.int32, (CHUNK, CHUNK), 0)
        col = lax.broadcasted_iota(jnp.int32, (CHUNK, CHUNK), 1)
        tril = (row >= col).astype(F32)
        for c in range((seq + pad) // CHUNK):
            rows = slice(c * CHUNK, (c + 1) * CHUNK)
            gp_ref[rows, :] = _dot(tril, gp_ref[rows, :], precision=HIGHEST)

    g_ref[...] = _expand_heads(gp_ref[...], j * heads_per_step, 0, heads_per_step, dk)
    b_ref[...] = _expand_heads(bp_ref[...], j * heads_per_step, n_heads, heads_per_step, dk)


def _gates_prompt(h3, w_t, ab_row0, layer, alog_row, dtb_row, pad, n_heads, dk):
    assert ab_row0 % LANES == 0
    b, seq, d = h3.shape
    t = seq + pad
    hps = 4
    kern = functools.partial(_gates_prompt_kernel, seq=seq, pad=pad, n_heads=n_heads, dk=dk, heads_per_step=hps)
    out = jax.ShapeDtypeStruct((b, t, n_heads * dk), F32)
    return _call(
        kern, grid=(b, n_heads // hps),
        in_specs=[pl.BlockSpec((None, seq, d), lambda i, j: (i, 0, 0), pipeline_mode=pl.Buffered(1)),
                  pl.BlockSpec((None, LANES, d), lambda i, j: (layer, ab_row0 // LANES, 0)),
                  pl.BlockSpec((1, LANES), lambda i, j: (0, 0)),
                  pl.BlockSpec((1, LANES), lambda i, j: (0, 0))],
        out_specs=[pl.BlockSpec((None, t, hps * dk), lambda i, j: (i, 0, j)),
                   pl.BlockSpec((None, t, hps * dk), lambda i, j: (i, 0, j))],
        out_shape=[out, out],
        scratch_shapes=[pltpu.VMEM((t, LANES), F32), pltpu.VMEM((t, LANES), F32)],
        name="gates_prompt")(h3, w_t, alog_row, dtb_row)


def _gates_sample_kernel(h_ref, wab_ref, alog_ref, dtb_ref, g_ref, b_ref, *, n_heads, dk):
    ab = _dot_nt(h_ref[...], wab_ref[...])
    g, beta = _gates_math(ab, alog_ref, dtb_ref)
    g_ref[...] = _expand_heads(g, 0, 0, n_heads, dk)
    b_ref[...] = _expand_heads(beta, 0, n_heads, n_heads, dk)


def _gates_sample(h2, w_t, ab_row0, layer, alog_row, dtb_row, n_heads, dk):
    bs, d = h2.shape
    out = jax.ShapeDtypeStruct((bs, n_heads * dk), F32)
    return _call(
        functools.partial(_gates_sample_kernel, n_heads=n_heads, dk=dk), grid=(1,),
        in_specs=[pl.BlockSpec((bs, d), lambda i: (0, 0)),
                  pl.BlockSpec((None, LANES, d), lambda i: (layer, ab_row0 // LANES, 0)),
                  pl.BlockSpec((1, LANES), lambda i: (0, 0)),
                  pl.BlockSpec((1, LANES), lambda i: (0, 0))],
        out_specs=[pl.BlockSpec((bs, n_heads * dk), lambda i: (0, 0)),
                   pl.BlockSpec((bs, n_heads * dk), lambda i: (0, 0))],
        out_shape=[out, out], name="gates_sample")(h2, w_t, alog_row, dtb_row)


def _mixb_prompt_kernel(h_ref, wb_ref, wc_ref, wh_ref, cw_ref, y_ref, nb_ref, ps_ref, *, seq, taps):
    h = h_ref[...]
    pre = _dot_nt(h, wc_ref[...]) * _dot_nt(h, wh_ref[...])
    ps_ref[0:SUBLANES, :] = jnp.zeros((SUBLANES, pre.shape[1]), F32)
    ps_ref[SUBLANES:SUBLANES + seq, :] = pre
    nb_ref[...] = ps_ref[SUBLANES + seq - (taps - 1):SUBLANES + seq, :]
    acc = pre * cw_ref[taps - 1:taps, :]
    for i in range(taps - 1):
        s = taps - 1 - i
        acc = acc + ps_ref[SUBLANES - s:SUBLANES - s + seq, :] * cw_ref[i:i + 1, :]
    y_ref[...] = (_dot_nt(h, wb_ref[...]) * acc).astype(y_ref.dtype)


def _mixb_prompt(h3, w_stack, cols, layer, cw):
    b, seq, d = h3.shape
    taps, width = cw.shape
    tn = MXU_COLS

    def wspec(col0):
        return pl.BlockSpec((None, tn, d), lambda i, j: (layer, j + col0 // tn, 0))

    return _call(
        functools.partial(_mixb_prompt_kernel, seq=seq, taps=taps), grid=(b, width // tn),
        in_specs=[pl.BlockSpec((None, seq, d), lambda i, j: (i, 0, 0), pipeline_mode=pl.Buffered(1)),
                  wspec(cols[0]), wspec(cols[1]), wspec(cols[2]), pl.BlockSpec((taps, tn), lambda i, j: (0, j))],
        out_specs=[pl.BlockSpec((None, seq, tn), lambda i, j: (i, 0, j)),
                   pl.BlockSpec((None, taps - 1, tn), lambda i, j: (i, 0, j))],
        out_shape=[jax.ShapeDtypeStruct((b, seq, width), BF16),
                   jax.ShapeDtypeStruct((b, taps - 1, width), F32)],
        scratch_shapes=[pltpu.VMEM((seq + SUBLANES, tn), F32)],
        name="mixb_prompt")(h3, w_stack, w_stack, w_stack, cw)


def _mixb_sample_kernel(h_ref, wb_ref, wc_ref, wh_ref, cw_ref, st_ref, y_ref, pre_ref, *, taps):
    h = h_ref[...]
    pre = _dot_nt(h, wc_ref[...]) * _dot_nt(h, wh_ref[...])
    pre_ref[...] = pre
    acc = pre * cw_ref[taps - 1:taps, :]
    for i in range(taps - 1):
        acc = acc + st_ref[i] * cw_ref[i:i + 1, :]
    y_ref[...] = (_dot_nt(h, wb_ref[...]) * acc).astype(y_ref.dtype)


def _mixb_sample(h2, w_stack, cols, layer, cw, st_stack):
    bs, d = h2.shape
    taps, width = cw.shape
    tn = MXU_COLS

    def wspec(col0):
        return pl.BlockSpec((None, tn, d), lambda j: (layer, j + col0 // tn, 0))

    return _call(
        functools.partial(_mixb_sample_kernel, taps=taps), grid=(width // tn,),
        in_specs=[pl.BlockSpec((bs, d), lambda j: (0, 0)), wspec(cols[0]), wspec(cols[1]), wspec(cols[2]),
                  pl.BlockSpec((taps, tn), lambda j: (0, j)),
                  pl.BlockSpec((None, taps - 1, bs, tn), lambda j: (layer, 0, 0, j))],
        out_specs=[pl.BlockSpec((bs, tn), lambda j: (0, j)), pl.BlockSpec((bs, tn), lambda j: (0, j))],
        out_shape=[jax.ShapeDtypeStruct((bs, width), BF16), jax.ShapeDtypeStruct((bs, width), F32)],
        name="mixb_sample")(h2, w_stack, w_stack, w_stack, cw, st_stack)


def _bmm(a, b):
    return jnp.einsum("cij,cjk->cik", a.astype(BF16), b.astype(BF16), preferred_element_type=F32)


def _bmm_nt(a, b):
    return jnp.einsum("cid,cjd->cij", a.astype(BF16), b.astype(BF16), preferred_element_type=F32)


def _unit_lower_inverse(lm, row, col, eye):
    blk16 = (row // 16) == (col // 16)
    blk32 = (row // 32) == (col // 32)
    ld = jnp.where(blk16, lm, 0.0)
    lo1 = jnp.where(jnp.logical_and(blk32, jnp.logical_not(blk16)), lm, 0.0)
    lo2 = jnp.where(blk32, 0.0, lm)
    x = eye.astype(F32) - ld
    m = _bmm(ld, ld)
    x = x + _bmm(x, m)
    m = _bmm(m, m)
    x = x + _bmm(x, m)
    m = _bmm(m, m)
    x = x + _bmm(x, m)
    x = x - _bmm(x, _bmm(lo1, x))
    x = x - _bmm(x, _bmm(lo2, x))
    return x


HEADS_PER_STEP = 2


def _delta_prompt_kernel(q_ref, k_ref, v_ref, g_ref, b_ref, go_ref, og_ref, o_ref, s_ref,
                         w_s, u_s, kt_s, qk_s, oo_s, *, seq, pad, dk):
    t = seq + pad
    nc = t // CHUNK
    row = lax.broadcasted_iota(jnp.int32, (CHUNK, CHUNK), 0)
    col = lax.broadcasted_iota(jnp.int32, (CHUNK, CHUNK), 1)
    eye = row == col
    incl = row >= col
    for hh in range(HEADS_PER_STEP):
        lanes = slice(hh * dk, (hh + 1) * dk)
        q3 = q_ref[:, lanes].reshape(nc, CHUNK, dk)
        k3 = k_ref[:, lanes].reshape(nc, CHUNK, dk)
        v3 = v_ref[:, lanes].reshape(nc, CHUNK, dk)
        g3 = g_ref[:, lanes].reshape(nc, CHUNK, dk)
        b3 = b_ref[:, lanes].reshape(nc, CHUNK, dk)
        kf = k3.astype(F32)
        gi = g3[:, :, :CHUNK]
        gj = jnp.sum(jnp.where(eye, gi, 0.0), axis=1, keepdims=True)
        decay = jnp.where(incl, jnp.exp(jnp.where(incl, gi - gj, 0.0)), 0.0)
        lm = jnp.where(row > col, b3[:, :, :CHUNK] * _bmm_nt(k3, k3) * decay, 0.0)
        tinv = _unit_lower_inverse(lm, row, col, eye)
        u_s[hh] = _bmm(tinv, b3 * v3).reshape(t, dk)
        w_s[hh] = _bmm(tinv, b3 * jnp.exp(g3) * kf).reshape(t, dk).astype(BF16)
        qk_s[hh] = (_bmm_nt(q3, k3) * decay).reshape(t, CHUNK).astype(BF16)
        kt_s[hh] = (kf * jnp.exp(g3[:, CHUNK - 1:CHUNK, :] - g3)).reshape(t, dk).astype(BF16)

    def body(c, states):
        r0 = pl.multiple_of(c * CHUNK, CHUNK)
        rows = pl.ds(r0, CHUNK)
        new_states = []
        for hh in range(HEADS_PER_STEP):
            lanes = slice(hh * dk, (hh + 1) * dk)
            s = states[hh]
            sb = s.astype(BF16)
            u = u_s[hh, rows, :] - _dot(w_s[hh, rows, :], sb)
            ub = u.astype(BF16)
            oo_s[hh, rows, :] = (jnp.exp(g_ref[rows, lanes]) * _dot(q_ref[rows, lanes], sb)
                                 + _dot(qk_s[hh, rows, :], ub))
            g_tail = g_ref[pl.ds(pl.multiple_of(r0 + CHUNK - SUBLANES, SUBLANES), SUBLANES), lanes]
            s_scale = jnp.exp(g_tail[SUBLANES - 1:SUBLANES, :])
            new_states.append(s_scale * s + lax.dot_general(kt_s[hh, rows, :], ub, (((0,), (0,)), ((), ())),
                                                            preferred_element_type=F32))
        return tuple(new_states)

    init = tuple(jnp.zeros((dk, dk), F32) for _ in range(HEADS_PER_STEP))
    states = lax.fori_loop(0, nc, body, init)
    for hh in range(HEADS_PER_STEP):
        lanes = slice(hh * dk, (hh + 1) * dk)
        s_ref[hh] = states[hh]
        o = oo_s[hh, pad:pad + seq, :]
        o = (o * lax.rsqrt(jnp.mean(o * o, axis=-1, keepdims=True) + EPS)) * og_ref[...]
        o_ref[:, lanes] = (o * _silu(go_ref[:, lanes])).astype(o_ref.dtype)


def _delta_prompt(qk, v, gcb, bcb, gout3, og_row, pad, n_heads, dk):
    b, t, _ = v.shape
    seq = t - pad
    hps = HEADS_PER_STEP
    wd = hps * dk

    def col_spec(rows, off):
        return pl.BlockSpec((None, rows, wd), lambda i, j: (i, 0, j + off))

    kern = functools.partial(_delta_prompt_kernel, seq=seq, pad=pad, dk=dk)
    return _call(
        kern, grid=(b, n_heads // hps),
        in_specs=[col_spec(t, 0), col_spec(t, n_heads // hps), col_spec(t, 0),
                  col_spec(t, 0), col_spec(t, 0), col_spec(seq, 0),
                  pl.BlockSpec((1, dk), lambda i, j: (0, 0))],
        out_specs=[col_spec(seq, 0), pl.BlockSpec((None, hps, dk, dk), lambda i, j: (i, j, 0, 0))],
        out_shape=[jax.ShapeDtypeStruct((b, seq, n_heads * dk), BF16),
                   jax.ShapeDtypeStruct((b, n_heads, dk, dk), F32)],
        scratch_shapes=[pltpu.VMEM((hps, t, dk), BF16), pltpu.VMEM((hps, t, dk), F32),
                        pltpu.VMEM((hps, t, dk), BF16), pltpu.VMEM((hps, t, CHUNK), BF16),
                        pltpu.VMEM((hps, t, dk), F32)],
        name="delta_prompt")(qk, qk, v, gcb, bcb, gout3, og_row)


def _delta_sample_kernel(q_ref, k_ref, v_ref, g_ref, b_ref, go_ref, og_ref, s_ref, *rest, n_heads, dk):
    o_ref, sn_ref = rest[-2:]
    q = q_ref[...]
    k = k_ref[...]
    v = v_ref[...]
    kq = jnp.concatenate([k, q, jnp.zeros((LANES - 2 * n_heads, dk), F32)], axis=0)
    kq_t = kq.T
    qk = jnp.sum(q * k, axis=-1, keepdims=True)
    outs = []
    for h in range(n_heads):
        s = s_ref[h]
        a = jnp.exp(g_ref[:, h * dk:(h + 1) * dk])
        beta = b_ref[:, h * dk:(h + 1) * dk]
        kcol = kq_t[:, h:h + 1]
        qcol = kq_t[:, n_heads + h:n_heads + h + 1]
        ks = jnp.sum(kcol * s, axis=0, keepdims=True)
        qs = jnp.sum(qcol * s, axis=0, keepdims=True)
        u = beta * (v[h:h + 1, :] - a * ks)
        outs.append(a * qs + qk[h:h + 1, :] * u)
        sn_ref[h] = a * s + kcol * u
    o = jnp.concatenate(outs, axis=0)
    o = (o * lax.rsqrt(jnp.mean(o * o, axis=-1, keepdims=True) + EPS)) * og_ref[...]
    o_ref[...] = (o * _silu(go_ref[...])).astype(o_ref.dtype)


def _delta_sample(q3, k3, v3, gcb, bcb, gout3, og_row, state, layer, prev_out):
    bs, n_heads, dk = q3.shape
    depth = state.shape[0]
    hspec = pl.BlockSpec((None, n_heads, dk), lambda i: (i, 0, 0))
    rspec = pl.BlockSpec((None, 1, n_heads * dk), lambda i: (i, 0, 0))
    sspec = pl.BlockSpec((None, None, n_heads, dk, dk), lambda i: (layer, i, 0, 0, 0))
    in_specs = [hspec, hspec, hspec, rspec, rspec, hspec, pl.BlockSpec((1, dk), lambda i: (0, 0)), sspec]
    args = [q3, k3, v3, gcb.reshape(bs, 1, -1), bcb.reshape(bs, 1, -1), gout3, og_row, state]
    aliases = {}
    if prev_out is not None:
        aliases = {len(args): 1}
        in_specs.append(pl.BlockSpec(memory_space=pl.ANY))
        args.append(prev_out)
    return _call(
        functools.partial(_delta_sample_kernel, n_heads=n_heads, dk=dk), grid=(bs,),
        in_specs=in_specs, out_specs=[hspec, sspec],
        out_shape=[jax.ShapeDtypeStruct((bs, n_heads, dk), BF16),
                   jax.ShapeDtypeStruct((depth, bs, n_heads, dk, dk), F32)],
        input_output_aliases=aliases, name="delta_sample")(*args)


def _merge_kernel(o_ref, y_ref, h_ref, wa_ref, wb_ref, wza_ref, wzb_ref, m_ref):
    h = h_ref[...]
    za = _sigmoid(_dot_nt(h, wza_ref[...]))
    zb = _sigmoid(_dot_nt(h, wzb_ref[...]))
    m_ref[...] = (za * _dot(o_ref[...], wa_ref[...]) + zb * _dot(y_ref[...], wb_ref[...])).astype(m_ref.dtype)


def _merge(o2, y2, h2, wa_stack, wb_stack, wz_stack, col_za, col_zb, layer, tm):
    m, d = h2.shape
    ka = o2.shape[1]
    kb = y2.shape[1]
    n = wa_stack.shape[-1]
    tn = MXU_COLS

    def wspec(kk):
        return pl.BlockSpec((None, kk, tn), lambda i, j: (layer, 0, j))

    def wtspec(row0):
        return pl.BlockSpec((None, tn, d), lambda i, j: (layer, j + row0 // tn, 0))

    return _call(
        _merge_kernel, grid=(m // tm, n // tn),
        in_specs=[pl.BlockSpec((tm, ka), lambda i, j: (i, 0)), pl.BlockSpec((tm, kb), lambda i, j: (i, 0)),
                  pl.BlockSpec((tm, d), lambda i, j: (i, 0)), wspec(ka), wspec(kb),
                  wtspec(col_za), wtspec(col_zb)],
        out_specs=pl.BlockSpec((tm, tn), lambda i, j: (i, j)),
        out_shape=jax.ShapeDtypeStruct((m, n), BF16), name="merge")(
            o2, y2, h2, wa_stack, wb_stack, wz_stack, wz_stack)


def _ffn_prompt_kernel(h_ref, wg_ref, wv_ref, cw_ref, cb_ref, a_ref, nb_ref, gs_ref, *, seq, taps):
    h = h_ref[...]
    gate = _dot(h, wg_ref[...].astype(BF16))
    gs_ref[0:SUBLANES, :] = jnp.zeros((SUBLANES, gate.shape[1]), F32)
    gs_ref[SUBLANES:SUBLANES + seq, :] = gate
    nb_ref[...] = gs_ref[SUBLANES + seq - (taps - 1):SUBLANES + seq, :]
    acc = gate * cw_ref[taps - 1:taps, :]
    for i in range(taps - 1):
        s = taps - 1 - i
        acc = acc + gs_ref[SUBLANES - s:SUBLANES - s + seq, :] * cw_ref[i:i + 1, :]
    a_ref[...] = (_silu(acc + cb_ref[...]) * _dot(h, wv_ref[...].astype(BF16))).astype(a_ref.dtype)


def _ffn_prompt(h3, wup_stack, layer, cw, cb_row):
    b, seq, d = h3.shape
    dff = cw.shape[1]
    taps = cw.shape[0]
    tn = MXU_COLS
    nt = dff // tn
    return _call(
        functools.partial(_ffn_prompt_kernel, seq=seq, taps=taps), grid=(b, nt),
        in_specs=[pl.BlockSpec((None, seq, d), lambda i, j: (i, 0, 0), pipeline_mode=pl.Buffered(1)),
                  pl.BlockSpec((None, d, tn), lambda i, j: (layer, 0, j)),
                  pl.BlockSpec((None, d, tn), lambda i, j: (layer, 0, j + nt)),
                  pl.BlockSpec((taps, tn), lambda i, j: (0, j)),
                  pl.BlockSpec((1, tn), lambda i, j: (0, j))],
        out_specs=[pl.BlockSpec((None, seq, tn), lambda i, j: (i, 0, j)),
                   pl.BlockSpec((None, taps - 1, tn), lambda i, j: (i, 0, j))],
        out_shape=[jax.ShapeDtypeStruct((b, seq, dff), BF16),
                   jax.ShapeDtypeStruct((b, taps - 1, dff), F32)],
        scratch_shapes=[pltpu.VMEM((seq + SUBLANES, tn), F32)],
        name="ffn_prompt")(h3, wup_stack, wup_stack, cw, cb_row)


def _ffn_sample_kernel(h_ref, wg_ref, wv_ref, cw_ref, cb_ref, st_ref, a_ref, g_ref, *, taps):
    h = h_ref[...]
    gate = _dot(h, wg_ref[...].astype(BF16))
    g_ref[...] = gate
    acc = gate * cw_ref[taps - 1:taps, :]
    for i in range(taps - 1):
        acc = acc + st_ref[i] * cw_ref[i:i + 1, :]
    a_ref[...] = (_silu(acc + cb_ref[...]) * _dot(h, wv_ref[...].astype(BF16))).astype(a_ref.dtype)


def _ffn_sample(h2, wup_stack, layer, cw, cb_row, st_stack):
    bs, d = h2.shape
    dff = cw.shape[1]
    taps = cw.shape[0]
    tn = MXU_COLS
    nt = dff // tn
    return _call(
        functools.partial(_ffn_sample_kernel, taps=taps), grid=(nt,),
        in_specs=[pl.BlockSpec((bs, d), lambda j: (0, 0)),
                  pl.BlockSpec((None, d, tn), lambda j: (layer, 0, j)),
                  pl.BlockSpec((None, d, tn), lambda j: (layer, 0, j + nt)),
                  pl.BlockSpec((taps, tn), lambda j: (0, j)),
                  pl.BlockSpec((1, tn), lambda j: (0, j)),
                  pl.BlockSpec((None, taps - 1, bs, tn), lambda j: (layer, 0, 0, j))],
        out_specs=[pl.BlockSpec((bs, tn), lambda j: (0, j)), pl.BlockSpec((bs, tn), lambda j: (0, j))],
        out_shape=[jax.ShapeDtypeStruct((bs, dff), BF16), jax.ShapeDtypeStruct((bs, dff), F32)],
        name="ffn_sample")(h2, wup_stack, wup_stack, cw, cb_row, st_stack)


def _cast_kernel(x_ref, o_ref):
    o_ref[...] = x_ref[0].astype(o_ref.dtype)


def _cast_rows_bf16(w_t, row0, rows):
    depth, _, k = w_t.shape
    tr = 512
    assert row0 % 16 == 0 and rows % tr == 0
    return _call(
        _cast_kernel, grid=(depth, rows // tr),
        in_specs=[pl.BlockSpec((pl.Element(1), pl.Element(tr), pl.Element(k)),
                               lambda l, i: (l, pl.multiple_of(row0 + i * tr, 16), 0))],
        out_specs=pl.BlockSpec((None, tr, k), lambda l, i: (l, i, 0)),
        out_shape=jax.ShapeDtypeStruct((depth, rows, k), BF16), name="cast_rows")(w_t)


def _pad_lanes(x2d):
    return jnp.pad(x2d, ((0, 0), (0, LANES - x2d.shape[1])))


def _roll_state(state_l, new_row):
    return jnp.concatenate([state_l[:, 1:], new_row[:, None]], axis=1)


def kernel(x_prompt, x_sample, state_delta, state_conv_qkv, state_conv_mix, state_conv_ffn, meta_tokens,
           norm1_g, w_in, conv_qkv_w, a_log, dt_bias, onorm_g, conv_mix_w, w_up_a, w_up_b, w_o, norm2_g,
           w_ffn_up, conv_ffn_w, conv_ffn_b, w_ffn_down, final_g):
    b, seq0, d = x_prompt.shape
    bs = x_sample.shape[0]
    depth, _, n_heads, dk, dv = state_delta.shape
    assert dk == dv == LANES and x_sample.shape[1] == 1
    n_meta = meta_tokens.shape[0]
    seq = seq0 + n_meta
    pad = (-seq) % CHUNK
    key_w = n_heads * dk
    val_w = n_heads * dv
    qkv_w = state_conv_qkv.shape[-1]
    w_b = state_conv_mix.shape[-1]
    d_ff = state_conv_ffn.shape[-1]
    assert 2 * n_heads <= LANES

    o_ab = qkv_w
    o_go = o_ab + 2 * n_heads
    o_bg = o_go + val_w
    o_cg = o_bg + w_b
    o_hb = o_cg + w_b
    o_za = o_hb + w_b
    o_zb = o_za + d
    assert o_zb + d == w_in.shape[-1]

    w_t = jnp.swapaxes(w_in, 1, 2)
    w_rest = _cast_rows_bf16(w_t, o_go, w_in.shape[-1] - o_go)
    c_go, c_bg, c_cg, c_hb, c_za, c_zb = (o - o_go for o in (o_go, o_bg, o_cg, o_hb, o_za, o_zb))
    wua, wub, wo, wfd = (w.astype(BF16) for w in (w_up_a, w_up_b, w_o, w_ffn_down))
    st_qkv = jnp.swapaxes(state_conv_qkv, 1, 2)
    st_mix = jnp.swapaxes(state_conv_mix, 1, 2)
    st_ffn = jnp.swapaxes(state_conv_ffn, 1, 2)

    xp = jnp.concatenate([jnp.broadcast_to(meta_tokens[None], (b, n_meta, d)), x_prompt], axis=1)
    xp = xp.reshape(b * seq, d)
    xs = x_sample.reshape(bs, d)
    m = b * seq
    tm = seq // 3
    rows_norm = 192
    assert seq % 3 == 0 and tm % 16 == 0 and m % rows_norm == 0
    tn_o = 512
    tn_f = MXU_COLS

    sp, qp, qs, mp, ms, fp, fs = [], [], [], [], [], [], []
    s_sample = None
    for l in range(depth):
        alog_row = _pad_lanes(a_log[l][None])
        dtb_row = _pad_lanes(dt_bias[l][None])
        og_row = onorm_g[l][None]
        cb_row = conv_ffn_b[l][None]

        h = _rmsnorm(xp, norm1_g[l][None], BF16, rows_norm)
        h3 = h.reshape(b, seq, d)
        qk, nb_qk = _qkv_prompt(h3, w_t, l, conv_qkv_w[l], pad, dk, 0, 2 * key_w,
                                key_w // MXU_COLS, key_w // MXU_COLS, BF16)
        v, nb_v = _qkv_prompt(h3, w_t, l, conv_qkv_w[l], pad, dk, 2 * key_w, val_w, 0, 0, F32)
        nb_q = jnp.concatenate([nb_qk, nb_v], axis=-1)
        gcb, bcb = _gates_prompt(h3, w_t, o_ab, l, alog_row, dtb_row, pad, n_heads, dk)
        gout = _mm(h, w_rest, l, c_go, val_w, tm, 1024, "gout")
        y_b, nb_m = _mixb_prompt(h3, w_rest, (c_bg, c_cg, c_hb), l, conv_mix_w[l])
        o, s_new = _delta_prompt(qk, v, gcb, bcb, gout.reshape(b, seq, val_w), og_row, pad, n_heads, dk)
        merged = _merge(o.reshape(m, val_w), y_b.reshape(m, w_b), h, wua, wub, w_rest, c_za, c_zb, l, tm)
        xp = _mm_res(merged, wo, l, xp, tm, tn_o, "out_proj")
        h2 = _rmsnorm(xp, norm2_g[l][None], BF16, rows_norm)
        act, nb_f = _ffn_prompt(h2.reshape(b, seq, d), w_ffn_up, l, conv_ffn_w[l], cb_row)
        xp = _mm_res(act.reshape(m, d_ff), wfd, l, xp, tm, tn_f, "ffn_down")
        sp.append(s_new); qp.append(nb_q); mp.append(nb_m); fp.append(nb_f)

        hs = _rmsnorm(xs, norm1_g[l][None], BF16, bs)
        qkv_s, z_new = _qkv_sample(hs, w_t, l, conv_qkv_w[l], st_qkv, dk, key_w)
        gcb_s, bcb_s = _gates_sample(hs, w_t, o_ab, l, alog_row, dtb_row, n_heads, dk)
        gout_s = _mm(hs, w_rest, l, c_go, val_w, bs, 1024, "gout")
        yb_s, pre_new = _mixb_sample(hs, w_rest, (c_bg, c_cg, c_hb), l, conv_mix_w[l], st_mix)
        q3 = qkv_s[:, :key_w].reshape(bs, n_heads, dk)
        k3 = qkv_s[:, key_w:2 * key_w].reshape(bs, n_heads, dk)
        v3 = qkv_s[:, 2 * key_w:].reshape(bs, n_heads, dv)
        o_s, s_sample = _delta_sample(q3, k3, v3, gcb_s, bcb_s, gout_s.reshape(bs, n_heads, dv), og_row,
                                      state_delta, l, s_sample)
        merged_s = _merge(o_s.reshape(bs, val_w), yb_s, hs, wua, wub, w_rest, c_za, c_zb, l, bs)
        xs = _mm_res(merged_s, wo, l, xs, bs, tn_o, "out_proj")
        h2s = _rmsnorm(xs, norm2_g[l][None], BF16, bs)
        act_s, gate_new = _ffn_sample(h2s, w_ffn_up, l, conv_ffn_w[l], cb_row, st_ffn)
        xs = _mm_res(act_s, wfd, l, xs, bs, tn_f, "ffn_down")
        qs.append(_roll_state(state_conv_qkv[l], z_new))
        ms.append(_roll_state(state_conv_mix[l], pre_new))
        fs.append(_roll_state(state_conv_ffn[l], gate_new))

    y_prompt = _rmsnorm(xp, final_g[None], F32, rows_norm).reshape(b, seq, d)[:, n_meta:]
    y_sample = _rmsnorm(xs, final_g[None], F32, bs).reshape(bs, 1, d)
    return (y_prompt, y_sample, jnp.stack(sp), s_sample, jnp.stack(qp), jnp.stack(qs),
            jnp.stack(mp), jnp.stack(ms), jnp.stack(fp), jnp.stack(fs))
```

```python
import functools

import jax
import jax.numpy as jnp
from jax import lax
from jax.experimental import pallas as pl
from jax.experimental.pallas import tpu as pltpu

F32 = jnp.float32
BF16 = jnp.bfloat16
EPS = 1e-6
HIGHEST = lax.Precision.HIGHEST
CHUNK = 64
LANES = 128
SUBLANES = 8
MXU_COLS = 256


def _sigmoid(x):
    return 1.0 / (1.0 + jnp.exp(-x))


def _silu(x):
    return x * _sigmoid(x)


def _softplus(x):
    return jnp.maximum(x, 0.0) + jnp.log1p(jnp.exp(-jnp.abs(x)))


def _dot(a, b, precision=None):
    return jnp.dot(a, b, preferred_element_type=F32, precision=precision)


def _dot_nt(a, wt):
    return lax.dot_general(a, wt.astype(BF16), (((1,), (1,)), ((), ())), preferred_element_type=F32)


def _call(kernel, *, grid, in_specs, out_specs, out_shape, scratch_shapes=(), name=None,
          input_output_aliases=None):
    return pl.pallas_call(
        kernel, grid=grid, in_specs=in_specs, out_specs=out_specs, out_shape=out_shape,
        scratch_shapes=list(scratch_shapes), name=name,
        input_output_aliases=input_output_aliases or {},
        compiler_params=pltpu.CompilerParams(dimension_semantics=("arbitrary",) * len(grid)))


def _rmsnorm_kernel(x_ref, g_ref, o_ref):
    x = x_ref[...]
    ms = jnp.mean(x * x, axis=-1, keepdims=True)
    o_ref[...] = ((x * lax.rsqrt(ms + EPS)) * g_ref[...]).astype(o_ref.dtype)


def _rmsnorm(x2d, g_row, out_dtype, rows):
    m, d = x2d.shape
    return _call(
        _rmsnorm_kernel, grid=(m // rows,),
        in_specs=[pl.BlockSpec((rows, d), lambda i: (i, 0)), pl.BlockSpec((1, d), lambda i: (0, 0))],
        out_specs=pl.BlockSpec((rows, d), lambda i: (i, 0)),
        out_shape=jax.ShapeDtypeStruct((m, d), out_dtype), name="rmsnorm")(x2d, g_row)


def _mm_kernel(a_ref, w_ref, o_ref):
    o_ref[...] = _dot_nt(a_ref[...], w_ref[...]).astype(o_ref.dtype)


def _mm(a, wt_stack, layer, row0, n, tm, tn, name):
    m, k = a.shape
    r0 = row0 // tn
    return _call(
        _mm_kernel, grid=(m // tm, n // tn),
        in_specs=[pl.BlockSpec((tm, k), lambda i, j: (i, 0)),
                  pl.BlockSpec((None, tn, k), lambda i, j: (layer, j + r0, 0))],
        out_specs=pl.BlockSpec((tm, tn), lambda i, j: (i, j)),
        out_shape=jax.ShapeDtypeStruct((m, n), F32), name=name)(a, wt_stack)


def _mm_res_kernel(a_ref, w_ref, r_ref, o_ref):
    o_ref[...] = r_ref[...] + _dot(a_ref[...], w_ref[...].astype(BF16))


def _mm_res(a, w_stack, layer, res, tm, tn, name):
    m, k = a.shape
    n = w_stack.shape[-1]
    return _call(
        _mm_res_kernel, grid=(m // tm, n // tn),
        in_specs=[pl.BlockSpec((tm, k), lambda i, j: (i, 0)),
                  pl.BlockSpec((None, k, tn), lambda i, j: (layer, 0, j)),
                  pl.BlockSpec((tm, tn), lambda i, j: (i, j))],
        out_specs=pl.BlockSpec((tm, tn), lambda i, j: (i, j)),
        out_shape=jax.ShapeDtypeStruct((m, n), F32), name=name)(a, w_stack, res)


def _l2norm_heads(x, dk, scale):
    parts = []
    for i in range(x.shape[1] // dk):
        p = x[:, i * dk:(i + 1) * dk]
        p = p * lax.rsqrt(jnp.sum(p * p, axis=-1, keepdims=True) + EPS)
        parts.append(p * scale if scale != 1.0 else p)
    return jnp.concatenate(parts, axis=-1)


def _qkv_finish(c, o_write, n, n_q, n_k, dk):
    if n_q == 0 and n_k == 0:
        o_write(c)
        return

    @pl.when(n < n_q)
    def _():
        o_write(_l2norm_heads(c, dk, dk ** -0.5))

    @pl.when(jnp.logical_and(n >= n_q, n < n_q + n_k))
    def _():
        o_write(_l2norm_heads(c, dk, 1.0))

    @pl.when(n >= n_q + n_k)
    def _():
        o_write(c)


def _conv_rows(zp_ref, cw_ref, seq, taps):
    acc = zp_ref[SUBLANES:SUBLANES + seq, :] * cw_ref[taps - 1:taps, :]
    for i in range(taps - 1):
        s = taps - 1 - i
        acc = acc + zp_ref[SUBLANES - s:SUBLANES - s + seq, :] * cw_ref[i:i + 1, :]
    return acc


def _lagged_tiles(step, n_tiles_total):
    return jnp.minimum(step, n_tiles_total - 1), jnp.maximum(step - 1, 0)


def _qkv_prompt_kernel(h_ref, w_ref, cw_ref, qkv_ref, nb_ref, zs_ref, *, seq, pad, taps, n_tiles, n_q, l2, dk):
    step = pl.program_id(0)

    @pl.when(step == 0)
    def _():
        zs_ref[...] = jnp.zeros(zs_ref.shape, F32)

    zp_ref = zs_ref.at[(step + 1) % 2]
    nb_ref[...] = zp_ref[SUBLANES + seq - (taps - 1):SUBLANES + seq, :]
    c = _silu(_conv_rows(zp_ref, cw_ref, seq, taps))
    if l2:
        n_e = jnp.maximum(step - 1, 0) % n_tiles
        c = _l2norm_heads(c, dk, 1.0) * jnp.where(n_e < n_q, dk ** -0.5, 1.0)
    qkv_ref[0:pad, :] = jnp.zeros((pad, c.shape[1]), qkv_ref.dtype)
    qkv_ref[pad:pad + seq, :] = c.astype(qkv_ref.dtype)
    zs_ref[step % 2, SUBLANES:SUBLANES + seq, :] = _dot_nt(h_ref[...], w_ref[...])


def _qkv_prompt(h3, w_t, layer, cw, pad, dk, col0, width, n_q, l2, out_dtype):
    b, seq, d = h3.shape
    taps = cw.shape[0]
    tn = MXU_COLS
    c0 = col0 // tn
    nt = width // tn
    total = b * nt

    def mm(s):
        return _lagged_tiles(s, total)[0]

    def ep(s):
        return _lagged_tiles(s, total)[1]

    kern = functools.partial(_qkv_prompt_kernel, seq=seq, pad=pad, taps=taps, n_tiles=nt, n_q=n_q, l2=l2, dk=dk)
    return _call(
        kern, grid=(total + 1,),
        in_specs=[pl.BlockSpec((None, seq, d), lambda s: (mm(s) // nt, 0, 0), pipeline_mode=pl.Buffered(1)),
                  pl.BlockSpec((None, tn, d), lambda s: (layer, mm(s) % nt + c0, 0)),
                  pl.BlockSpec((taps, tn), lambda s: (0, ep(s) % nt + c0))],
        out_specs=[pl.BlockSpec((None, seq + pad, tn), lambda s: (ep(s) // nt, 0, ep(s) % nt)),
                   pl.BlockSpec((None, taps - 1, tn), lambda s: (ep(s) // nt, 0, ep(s) % nt))],
        out_shape=[jax.ShapeDtypeStruct((b, seq + pad, width), out_dtype),
                   jax.ShapeDtypeStruct((b, taps - 1, width), F32)],
        scratch_shapes=[pltpu.VMEM((2, seq + SUBLANES, tn), F32)], name="qkv_prompt")(h3, w_t, cw)


def _qkv_sample_kernel(h_ref, w_ref, cw_ref, st_ref, qkv_ref, z_ref, *, taps, n_q, n_k, dk):
    n = pl.program_id(0)
    z = _dot_nt(h_ref[...], w_ref[...])
    z_ref[...] = z
    acc = z * cw_ref[taps - 1:taps, :]
    for i in range(taps - 1):
        acc = acc + st_ref[i] * cw_ref[i:i + 1, :]
    c = _silu(acc)

    def write(val):
        qkv_ref[...] = val

    _qkv_finish(c, write, n, n_q, n_k, dk)


def _qkv_sample(h2, w_stack, layer, cw, st_stack, dk, key_w):
    bs, d = h2.shape
    taps, width = cw.shape
    tn = MXU_COLS
    kern = functools.partial(_qkv_sample_kernel, taps=taps, n_q=key_w // tn, n_k=key_w // tn, dk=dk)
    return _call(
        kern, grid=(width // tn,),
        in_specs=[pl.BlockSpec((bs, d), lambda j: (0, 0)),
                  pl.BlockSpec((None, tn, d), lambda j: (layer, j, 0)),
                  pl.BlockSpec((taps, tn), lambda j: (0, j)),
                  pl.BlockSpec((None, taps - 1, bs, tn), lambda j: (layer, 0, 0, j))],
        out_specs=[pl.BlockSpec((bs, tn), lambda j: (0, j)), pl.BlockSpec((bs, tn), lambda j: (0, j))],
        out_shape=[jax.ShapeDtypeStruct((bs, width), F32), jax.ShapeDtypeStruct((bs, width), F32)],
        name="qkv_sample")(h2, w_stack, cw, st_stack)


def _gates_math(ab, alog_ref, dtb_ref):
    g = -jnp.exp(alog_ref[...]) * _softplus(ab + dtb_ref[...])
    return g, _sigmoid(ab)


def _expand_heads(x, first_head, lane_off, heads, dk):
    src = lax.broadcasted_iota(jnp.int32, (LANES, heads * dk), 0)
    dst = lax.broadcasted_iota(jnp.int32, (LANES, heads * dk), 1) // dk + (first_head + lane_off)
    return _dot(x, (src == dst).astype(F32), precision=HIGHEST)


def _gates_prompt_kernel(h_ref, wab_ref, alog_ref, dtb_ref, g_ref, b_ref, gp_ref, bp_ref,
                         *, seq, pad, n_heads, dk, heads_per_step):
    j = pl.program_id(1)

    @pl.when(j == 0)
    def _():
        ab = _dot_nt(h_ref[...], wab_ref[...])
        g, beta = _gates_math(ab, alog_ref, dtb_ref)
        zeros = jnp.zeros((pad, LANES), F32)
        gp_ref[0:pad, :] = zeros
        bp_ref[0:pad, :] = zeros
        gp_ref[pad:pad + seq, :] = g
        bp_ref[pad:pad + seq, :] = beta
        row = lax.broadcasted_iota(jnp.int32, (CHUNK, CHUNK), 0)
        col = lax.broadcasted_iota(jnp.int32, (CHUNK, CHUNK), 1)
        tril = (row >= col).astype(F32)
        for c in range((seq + pad) // CHUNK):
            rows = slice(c * CHUNK, (c + 1) * CHUNK)
            gp_ref[rows, :] = _dot(tril, gp_ref[rows, :], precision=HIGHEST)

    g_ref[...] = _expand_heads(gp_ref[...], j * heads_per_step, 0, heads_per_step, dk)
    b_ref[...] = _expand_heads(bp_ref[...], j * heads_per_step, n_heads, heads_per_step, dk)


def _gates_prompt(h3, w_t, ab_row0, layer, alog_row, dtb_row, pad, n_heads, dk):
    assert ab_row0 % LANES == 0
    b, seq, d = h3.shape
    t = seq + pad
    hps = 4
    kern = functools.partial(_gates_prompt_kernel, seq=seq, pad=pad, n_heads=n_heads, dk=dk, heads_per_step=hps)
    out = jax.ShapeDtypeStruct((b, t, n_heads * dk), F32)
    return _call(
        kern, grid=(b, n_heads // hps),
        in_specs=[pl.BlockSpec((None, seq, d), lambda i, j: (i, 0, 0), pipeline_mode=pl.Buffered(1)),
                  pl.BlockSpec((None, LANES, d), lambda i, j: (layer, ab_row0 // LANES, 0)),
                  pl.BlockSpec((1, LANES), lambda i, j: (0, 0)),
                  pl.BlockSpec((1, LANES), lambda i, j: (0, 0))],
        out_specs=[pl.BlockSpec((None, t, hps * dk), lambda i, j: (i, 0, j)),
                   pl.BlockSpec((None, t, hps * dk), lambda i, j: (i, 0, j))],
        out_shape=[out, out],
        scratch_shapes=[pltpu.VMEM((t, LANES), F32), pltpu.VMEM((t, LANES), F32)],
        name="gates_prompt")(h3, w_t, alog_row, dtb_row)


def _gates_sample_kernel(h_ref, wab_ref, alog_ref, dtb_ref, g_ref, b_ref, *, n_heads, dk):
    ab = _dot_nt(h_ref[...], wab_ref[...])
    g, beta = _gates_math(ab, alog_ref, dtb_ref)
    g_ref[...] = _expand_heads(g, 0, 0, n_heads, dk)
    b_ref[...] = _expand_heads(beta, 0, n_heads, n_heads, dk)


def _gates_sample(h2, w_t, ab_row0, layer, alog_row, dtb_row, n_heads, dk):
    bs, d = h2.shape
    out = jax.ShapeDtypeStruct((bs, n_heads * dk), F32)
    return _call(
        functools.partial(_gates_sample_kernel, n_heads=n_heads, dk=dk), grid=(1,),
        in_specs=[pl.BlockSpec((bs, d), lambda i: (0, 0)),
                  pl.BlockSpec((None, LANES, d), lambda i: (layer, ab_row0 // LANES, 0)),
                  pl.BlockSpec((1, LANES), lambda i: (0, 0)),
                  pl.BlockSpec((1, LANES), lambda i: (0, 0))],
        out_specs=[pl.BlockSpec((bs, n_heads * dk), lambda i: (0, 0)),
                   pl.BlockSpec((bs, n_heads * dk), lambda i: (0, 0))],
        out_shape=[out, out], name="gates_sample")(h2, w_t, alog_row, dtb_row)


def _mixb_prompt_kernel(h_ref, wb_ref, wc_ref, wh_ref, cw_ref, y_ref, nb_ref, ps_ref, *, seq, taps):
    h = h_ref[...]
    pre = _dot_nt(h, wc_ref[...]) * _dot_nt(h, wh_ref[...])
    ps_ref[0:SUBLANES, :] = jnp.zeros((SUBLANES, pre.shape[1]), F32)
    ps_ref[SUBLANES:SUBLANES + seq, :] = pre
    nb_ref[...] = ps_ref[SUBLANES + seq - (taps - 1):SUBLANES + seq, :]
    acc = pre * cw_ref[taps - 1:taps, :]
    for i in range(taps - 1):
        s = taps - 1 - i
        acc = acc + ps_ref[SUBLANES - s:SUBLANES - s + seq, :] * cw_ref[i:i + 1, :]
    y_ref[...] = (_dot_nt(h, wb_ref[...]) * acc).astype(y_ref.dtype)


def _mixb_prompt(h3, w_stack, cols, layer, cw):
    b, seq, d = h3.shape
    taps, width = cw.shape
    tn = MXU_COLS

    def wspec(col0):
        return pl.BlockSpec((None, tn, d), lambda i, j: (layer, j + col0 // tn, 0))

    return _call(
        functools.partial(_mixb_prompt_kernel, seq=seq, taps=taps), grid=(b, width // tn),
        in_specs=[pl.BlockSpec((None, seq, d), lambda i, j: (i, 0, 0), pipeline_mode=pl.Buffered(1)),
                  wspec(cols[0]), wspec(cols[1]), wspec(cols[2]), pl.BlockSpec((taps, tn), lambda i, j: (0, j))],
        out_specs=[pl.BlockSpec((None, seq, tn), lambda i, j: (i, 0, j)),
                   pl.BlockSpec((None, taps - 1, tn), lambda i, j: (i, 0, j))],
        out_shape=[jax.ShapeDtypeStruct((b, seq, width), BF16),
                   jax.ShapeDtypeStruct((b, taps - 1, width), F32)],
        scratch_shapes=[pltpu.VMEM((seq + SUBLANES, tn), F32)],
        name="mixb_prompt")(h3, w_stack, w_stack, w_stack, cw)


def _mixb_sample_kernel(h_ref, wb_ref, wc_ref, wh_ref, cw_ref, st_ref, y_ref, pre_ref, *, taps):
    h = h_ref[...]
    pre = _dot_nt(h, wc_ref[...]) * _dot_nt(h, wh_ref[...])
    pre_ref[...] = pre
    acc = pre * cw_ref[taps - 1:taps, :]
    for i in range(taps - 1):
        acc = acc + st_ref[i] * cw_ref[i:i + 1, :]
    y_ref[...] = (_dot_nt(h, wb_ref[...]) * acc).astype(y_ref.dtype)


def _mixb_sample(h2, w_stack, cols, layer, cw, st_stack):
    bs, d = h2.shape
    taps, width = cw.shape
    tn = MXU_COLS

    def wspec(col0):
        return pl.BlockSpec((None, tn, d), lambda j: (layer, j + col0 // tn, 0))

    return _call(
        functools.partial(_mixb_sample_kernel, taps=taps), grid=(width // tn,),
        in_specs=[pl.BlockSpec((bs, d), lambda j: (0, 0)), wspec(cols[0]), wspec(cols[1]), wspec(cols[2]),
                  pl.BlockSpec((taps, tn), lambda j: (0, j)),
                  pl.BlockSpec((None, taps - 1, bs, tn), lambda j: (layer, 0, 0, j))],
        out_specs=[pl.BlockSpec((bs, tn), lambda j: (0, j)), pl.BlockSpec((bs, tn), lambda j: (0, j))],
        out_shape=[jax.ShapeDtypeStruct((bs, width), BF16), jax.ShapeDtypeStruct((bs, width), F32)],
        name="mixb_sample")(h2, w_stack, w_stack, w_stack, cw, st_stack)


def _bmm(a, b):
    return jnp.einsum("cij,cjk->cik", a.astype(BF16), b.astype(BF16), preferred_element_type=F32)


def _bmm_nt(a, b):
    return jnp.einsum("cid,cjd->cij", a.astype(BF16), b.astype(BF16), preferred_element_type=F32)


def _unit_lower_inverse(lm, row, col, eye):
    blk16 = (row // 16) == (col // 16)
    blk32 = (row // 32) == (col // 32)
    ld = jnp.where(blk16, lm, 0.0)
    lo1 = jnp.where(jnp.logical_and(blk32, jnp.logical_not(blk16)), lm, 0.0)
    lo2 = jnp.where(blk32, 0.0, lm)
    x = eye.astype(F32) - ld
    m = _bmm(ld, ld)
    x = x + _bmm(x, m)
    m = _bmm(m, m)
    x = x + _bmm(x, m)
    m = _bmm(m, m)
    x = x + _bmm(x, m)
    x = x - _bmm(x, _bmm(lo1, x))
    x = x - _bmm(x, _bmm(lo2, x))
    return x


HEADS_PER_STEP = 2


def _delta_prompt_kernel(q_ref, k_ref, v_ref, g_ref, b_ref, go_ref, og_ref, o_ref, s_ref,
                         w_s, u_s, kt_s, qk_s, oo_s, *, seq, pad, dk):
    t = seq + pad
    nc = t // CHUNK
    row = lax.broadcasted_iota(jnp.int32, (CHUNK, CHUNK), 0)
    col = lax.broadcasted_iota(jnp.int32, (CHUNK, CHUNK), 1)
    eye = row == col
    incl = row >= col
    for hh in range(HEADS_PER_STEP):
        lanes = slice(hh * dk, (hh + 1) * dk)
        q3 = q_ref[:, lanes].reshape(nc, CHUNK, dk)
        k3 = k_ref[:, lanes].reshape(nc, CHUNK, dk)
        v3 = v_ref[:, lanes].reshape(nc, CHUNK, dk)
        g3 = g_ref[:, lanes].reshape(nc, CHUNK, dk)
        b3 = b_ref[:, lanes].reshape(nc, CHUNK, dk)
        kf = k3.astype(F32)
        gi = g3[:, :, :CHUNK]
        gj = jnp.sum(jnp.where(eye, gi, 0.0), axis=1, keepdims=True)
        decay = jnp.where(incl, jnp.exp(jnp.where(incl, gi - gj, 0.0)), 0.0)
        lm = jnp.where(row > col, b3[:, :, :CHUNK] * _bmm_nt(k3, k3) * decay, 0.0)
        tinv = _unit_lower_inverse(lm, row, col, eye)
        u_s[hh] = _bmm(tinv, b3 * v3).reshape(t, dk)
        w_s[hh] = _bmm(tinv, b3 * jnp.exp(g3) * kf).reshape(t, dk).astype(BF16)
        qk_s[hh] = (_bmm_nt(q3, k3) * decay).reshape(t, CHUNK).astype(BF16)
        kt_s[hh] = (kf * jnp.exp(g3[:, CHUNK - 1:CHUNK, :] - g3)).reshape(t, dk).astype(BF16)

    def body(c, states):
        r0 = pl.multiple_of(c * CHUNK, CHUNK)
        rows = pl.ds(r0, CHUNK)
        new_states = []
        for hh in range(HEADS_PER_STEP):
            lanes = slice(hh * dk, (hh + 1) * dk)
            s = states[hh]
            sb = s.astype(BF16)
            u = u_s[hh, rows, :] - _dot(w_s[hh, rows, :], sb)
            ub = u.astype(BF16)
            oo_s[hh, rows, :] = (jnp.exp(g_ref[rows, lanes]) * _dot(q_ref[rows, lanes], sb)
                                 + _dot(qk_s[hh, rows, :], ub))
            g_tail = g_ref[pl.ds(pl.multiple_of(r0 + CHUNK - SUBLANES, SUBLANES), SUBLANES), lanes]
            s_scale = jnp.exp(g_tail[SUBLANES - 1:SUBLANES, :])
            new_states.append(s_scale * s + lax.dot_general(kt_s[hh, rows, :], ub, (((0,), (0,)), ((), ())),
                                                            preferred_element_type=F32))
        return tuple(new_states)

    init = tuple(jnp.zeros((dk, dk), F32) for _ in range(HEADS_PER_STEP))
    states = lax.fori_loop(0, nc, body, init)
    for hh in range(HEADS_PER_STEP):
        lanes = slice(hh * dk, (hh + 1) * dk)
        s_ref[hh] = states[hh]
        o = oo_s[hh, pad:pad + seq, :]
        o = (o * lax.rsqrt(jnp.mean(o * o, axis=-1, keepdims=True) + EPS)) * og_ref[...]
        o_ref[:, lanes] = (o * _silu(go_ref[:, lanes])).astype(o_ref.dtype)


def _delta_prompt(qk, v, gcb, bcb, gout3, og_row, pad, n_heads, dk):
    b, t, _ = v.shape
    seq = t - pad
    hps = HEADS_PER_STEP
    wd = hps * dk

    def col_spec(rows, off):
        return pl.BlockSpec((None, rows, wd), lambda i, j: (i, 0, j + off))

    kern = functools.partial(_delta_prompt_kernel, seq=seq, pad=pad, dk=dk)
    return _call(
        kern, grid=(b, n_heads // hps),
        in_specs=[col_spec(t, 0), col_spec(t, n_heads // hps), col_spec(t, 0),
                  col_spec(t, 0), col_spec(t, 0), col_spec(seq, 0),
                  pl.BlockSpec((1, dk), lambda i, j: (0, 0))],
        out_specs=[col_spec(seq, 0), pl.BlockSpec((None, hps, dk, dk), lambda i, j: (i, j, 0, 0))],
        out_shape=[jax.ShapeDtypeStruct((b, seq, n_heads * dk), BF16),
                   jax.ShapeDtypeStruct((b, n_heads, dk, dk), F32)],
        scratch_shapes=[pltpu.VMEM((hps, t, dk), BF16), pltpu.VMEM((hps, t, dk), F32),
                        pltpu.VMEM((hps, t, dk), BF16), pltpu.VMEM((hps, t, CHUNK), BF16),
                        pltpu.VMEM((hps, t, dk), F32)],
        name="delta_prompt")(qk, qk, v, gcb, bcb, gout3, og_row)


def _delta_sample_kernel(q_ref, k_ref, v_ref, g_ref, b_ref, go_ref, og_ref, s_ref, *rest, n_heads, dk):
    o_ref, sn_ref = rest[-2:]
    for bi in range(SAMPLES_PER_STEP):
        q = q_ref[bi]
        k = k_ref[bi]
        v = v_ref[bi]
        kq = jnp.concatenate([k, q, jnp.zeros((LANES - 2 * n_heads, dk), F32)], axis=0)
        kq_t = kq.T
        qk = jnp.sum(q * k, axis=-1, keepdims=True)
        outs = []
        for h in range(n_heads):
            s = s_ref[bi, h]
            a = jnp.exp(g_ref[bi, :, h * dk:(h + 1) * dk])
            beta = b_ref[bi, :, h * dk:(h + 1) * dk]
            kcol = kq_t[:, h:h + 1]
            qcol = kq_t[:, n_heads + h:n_heads + h + 1]
            ks = jnp.sum(kcol * s, axis=0, keepdims=True)
            qs = jnp.sum(qcol * s, axis=0, keepdims=True)
            u = beta * (v[h:h + 1, :] - a * ks)
            outs.append(a * qs + qk[h:h + 1, :] * u)
            sn_ref[bi, h] = a * s + kcol * u
        o = jnp.concatenate(outs, axis=0)
        o = (o * lax.rsqrt(jnp.mean(o * o, axis=-1, keepdims=True) + EPS)) * og_ref[...]
        o_ref[bi] = (o * _silu(go_ref[bi])).astype(o_ref.dtype)


SAMPLES_PER_STEP = 2


def _delta_sample(q3, k3, v3, gcb, bcb, gout3, og_row, state, layer, prev_out):
    bs, n_heads, dk = q3.shape
    depth = state.shape[0]
    nb = SAMPLES_PER_STEP
    hspec = pl.BlockSpec((nb, n_heads, dk), lambda i: (i, 0, 0))
    rspec = pl.BlockSpec((nb, 1, n_heads * dk), lambda i: (i, 0, 0))
    sspec = pl.BlockSpec((None, nb, n_heads, dk, dk), lambda i: (layer, i, 0, 0, 0))
    in_specs = [hspec, hspec, hspec, rspec, rspec, hspec, pl.BlockSpec((1, dk), lambda i: (0, 0)), sspec]
    args = [q3, k3, v3, gcb.reshape(bs, 1, -1), bcb.reshape(bs, 1, -1), gout3, og_row, state]
    aliases = {}
    if prev_out is not None:
        aliases = {len(args): 1}
        in_specs.append(pl.BlockSpec(memory_space=pl.ANY))
        args.append(prev_out)
    return _call(
        functools.partial(_delta_sample_kernel, n_heads=n_heads, dk=dk), grid=(bs // nb,),
        in_specs=in_specs, out_specs=[hspec, sspec],
        out_shape=[jax.ShapeDtypeStruct((bs, n_heads, dk), BF16),
                   jax.ShapeDtypeStruct((depth, bs, n_heads, dk, dk), F32)],
        input_output_aliases=aliases, name="delta_sample")(*args)


def _merge_kernel(o_ref, y_ref, h_ref, wa_ref, wb_ref, wza_ref, wzb_ref, m_ref):
    h = h_ref[...]
    za = _sigmoid(_dot_nt(h, wza_ref[...]))
    zb = _sigmoid(_dot_nt(h, wzb_ref[...]))
    pa = _dot(o_ref[...], wa_ref[...].astype(BF16))
    pb = _dot(y_ref[...], wb_ref[...].astype(BF16))
    m_ref[...] = (za * pa + zb * pb).astype(m_ref.dtype)


def _merge(o2, y2, h2, wa_stack, wb_stack, wz_stack, col_za, col_zb, layer, tm):
    m, d = h2.shape
    ka = o2.shape[1]
    kb = y2.shape[1]
    n = wa_stack.shape[-1]
    tn = MXU_COLS

    def wspec(kk):
        return pl.BlockSpec((None, kk, tn), lambda i, j: (layer, 0, j))

    def wtspec(row0):
        return pl.BlockSpec((None, tn, d), lambda i, j: (layer, j + row0 // tn, 0))

    return _call(
        _merge_kernel, grid=(m // tm, n // tn),
        in_specs=[pl.BlockSpec((tm, ka), lambda i, j: (i, 0)), pl.BlockSpec((tm, kb), lambda i, j: (i, 0)),
                  pl.BlockSpec((tm, d), lambda i, j: (i, 0)), wspec(ka), wspec(kb),
                  wtspec(col_za), wtspec(col_zb)],
        out_specs=pl.BlockSpec((tm, tn), lambda i, j: (i, j)),
        out_shape=jax.ShapeDtypeStruct((m, n), BF16), name="merge")(
            o2, y2, h2, wa_stack, wb_stack, wz_stack, wz_stack)


def _ffn_prompt_kernel(h_ref, wg_ref, wv_ref, cw_ref, cb_ref, a_ref, nb_ref, gs_ref, *, seq, taps):
    h = h_ref[...]
    gs_ref[0:SUBLANES, :] = jnp.zeros((SUBLANES, gs_ref.shape[1]), F32)
    gs_ref[SUBLANES:SUBLANES + seq, :] = _dot(h, wg_ref[...].astype(BF16))
    nb_ref[...] = gs_ref[SUBLANES + seq - (taps - 1):SUBLANES + seq, :]
    acc = _conv_rows(gs_ref, cw_ref, seq, taps)
    a_ref[...] = (_silu(acc + cb_ref[...]) * _dot(h, wv_ref[...].astype(BF16))).astype(a_ref.dtype)


def _ffn_prompt(h3, wup_stack, layer, cw, cb_row):
    b, seq, d = h3.shape
    dff = cw.shape[1]
    taps = cw.shape[0]
    tn = MXU_COLS
    nt = dff // tn
    return _call(
        functools.partial(_ffn_prompt_kernel, seq=seq, taps=taps), grid=(b, nt),
        in_specs=[pl.BlockSpec((None, seq, d), lambda i, j: (i, 0, 0), pipeline_mode=pl.Buffered(1)),
                  pl.BlockSpec((None, d, tn), lambda i, j: (layer, 0, j)),
                  pl.BlockSpec((None, d, tn), lambda i, j: (layer, 0, j + nt)),
                  pl.BlockSpec((taps, tn), lambda i, j: (0, j)),
                  pl.BlockSpec((1, tn), lambda i, j: (0, j))],
        out_specs=[pl.BlockSpec((None, seq, tn), lambda i, j: (i, 0, j)),
                   pl.BlockSpec((None, taps - 1, tn), lambda i, j: (i, 0, j))],
        out_shape=[jax.ShapeDtypeStruct((b, seq, dff), BF16),
                   jax.ShapeDtypeStruct((b, taps - 1, dff), F32)],
        scratch_shapes=[pltpu.VMEM((seq + SUBLANES, tn), F32)],
        name="ffn_prompt")(h3, wup_stack, wup_stack, cw, cb_row)


def _ffn_sample_kernel(h_ref, wg_ref, wv_ref, cw_ref, cb_ref, st_ref, a_ref, g_ref, *, taps):
    h = h_ref[...]
    gate = _dot(h, wg_ref[...].astype(BF16))
    g_ref[...] = gate
    acc = gate * cw_ref[taps - 1:taps, :]
    for i in range(taps - 1):
        acc = acc + st_ref[i] * cw_ref[i:i + 1, :]
    a_ref[...] = (_silu(acc + cb_ref[...]) * _dot(h, wv_ref[...].astype(BF16))).astype(a_ref.dtype)


def _ffn_sample(h2, wup_stack, layer, cw, cb_row, st_stack):
    bs, d = h2.shape
    dff = cw.shape[1]
    taps = cw.shape[0]
    tn = MXU_COLS
    nt = dff // tn
    return _call(
        functools.partial(_ffn_sample_kernel, taps=taps), grid=(nt,),
        in_specs=[pl.BlockSpec((bs, d), lambda j: (0, 0)),
                  pl.BlockSpec((None, d, tn), lambda j: (layer, 0, j)),
                  pl.BlockSpec((None, d, tn), lambda j: (layer, 0, j + nt)),
                  pl.BlockSpec((taps, tn), lambda j: (0, j)),
                  pl.BlockSpec((1, tn), lambda j: (0, j)),
                  pl.BlockSpec((None, taps - 1, bs, tn), lambda j: (layer, 0, 0, j))],
        out_specs=[pl.BlockSpec((bs, tn), lambda j: (0, j)), pl.BlockSpec((bs, tn), lambda j: (0, j))],
        out_shape=[jax.ShapeDtypeStruct((bs, dff), BF16), jax.ShapeDtypeStruct((bs, dff), F32)],
        name="ffn_sample")(h2, wup_stack, wup_stack, cw, cb_row, st_stack)


def _cast_kernel(x_ref, o_ref):
    o_ref[...] = x_ref[0].astype(o_ref.dtype)


def _cast_rows_bf16(w_t, row0, rows):
    depth, _, k = w_t.shape
    tr = 512
    assert row0 % 16 == 0 and rows % tr == 0
    return _call(
        _cast_kernel, grid=(depth, rows // tr),
        in_specs=[pl.BlockSpec((pl.Element(1), pl.Element(tr), pl.Element(k)),
                               lambda l, i: (l, pl.multiple_of(row0 + i * tr, 16), 0))],
        out_specs=pl.BlockSpec((None, tr, k), lambda l, i: (l, i, 0)),
        out_shape=jax.ShapeDtypeStruct((depth, rows, k), BF16), name="cast_rows")(w_t)


def _pad_lanes(x2d):
    return jnp.pad(x2d, ((0, 0), (0, LANES - x2d.shape[1])))


def _roll_state(state_l, new_row):
    return jnp.concatenate([state_l[:, 1:], new_row[:, None]], axis=1)


def kernel(x_prompt, x_sample, state_delta, state_conv_qkv, state_conv_mix, state_conv_ffn, meta_tokens,
           norm1_g, w_in, conv_qkv_w, a_log, dt_bias, onorm_g, conv_mix_w, w_up_a, w_up_b, w_o, norm2_g,
           w_ffn_up, conv_ffn_w, conv_ffn_b, w_ffn_down, final_g):
    b, seq0, d = x_prompt.shape
    bs = x_sample.shape[0]
    depth, _, n_heads, dk, dv = state_delta.shape
    assert dk == dv == LANES and x_sample.shape[1] == 1
    n_meta = meta_tokens.shape[0]
    seq = seq0 + n_meta
    pad = (-seq) % CHUNK
    key_w = n_heads * dk
    val_w = n_heads * dv
    qkv_w = state_conv_qkv.shape[-1]
    w_b = state_conv_mix.shape[-1]
    d_ff = state_conv_ffn.shape[-1]
    assert 2 * n_heads <= LANES

    o_ab = qkv_w
    o_go = o_ab + 2 * n_heads
    o_bg = o_go + val_w
    o_cg = o_bg + w_b
    o_hb = o_cg + w_b
    o_za = o_hb + w_b
    o_zb = o_za + d
    assert o_zb + d == w_in.shape[-1]

    w_t = jnp.swapaxes(w_in, 1, 2)
    w_rest = _cast_rows_bf16(w_t, o_go, w_in.shape[-1] - o_go)
    c_go, c_bg, c_cg, c_hb, c_za, c_zb = (o - o_go for o in (o_go, o_bg, o_cg, o_hb, o_za, o_zb))
    wua, wub, wo = w_up_a, w_up_b, w_o
    wfd = w_ffn_down.astype(BF16)
    st_qkv = jnp.swapaxes(state_conv_qkv, 1, 2)
    st_mix = jnp.swapaxes(state_conv_mix, 1, 2)
    st_ffn = jnp.swapaxes(state_conv_ffn, 1, 2)

    xp = jnp.concatenate([jnp.broadcast_to(meta_tokens[None], (b, n_meta, d)), x_prompt], axis=1)
    xp = xp.reshape(b * seq, d)
    xs = x_sample.reshape(bs, d)
    m = b * seq
    tm = seq // 3
    rows_norm = tm
    assert seq % 3 == 0 and tm % 16 == 0
    tn_o = 512
    tn_f = MXU_COLS

    sp, qp, qs, mp, ms, fp, fs = [], [], [], [], [], [], []
    s_sample = None
    for l in range(depth):
        alog_row = _pad_lanes(a_log[l][None])
        dtb_row = _pad_lanes(dt_bias[l][None])
        og_row = onorm_g[l][None]
        cb_row = conv_ffn_b[l][None]

        h = _rmsnorm(xp, norm1_g[l][None], BF16, rows_norm)
        h3 = h.reshape(b, seq, d)
        qk, nb_qk = _qkv_prompt(h3, w_t, l, conv_qkv_w[l], pad, dk, 0, 2 * key_w, key_w // MXU_COLS, True, BF16)
        v, nb_v = _qkv_prompt(h3, w_t, l, conv_qkv_w[l], pad, dk, 2 * key_w, val_w, 0, False, F32)
        nb_q = jnp.concatenate([nb_qk, nb_v], axis=-1)
        gcb, bcb = _gates_prompt(h3, w_t, o_ab, l, alog_row, dtb_row, pad, n_heads, dk)
        gout = _mm(h, w_rest, l, c_go, val_w, tm, 1024, "gout")
        y_b, nb_m = _mixb_prompt(h3, w_rest, (c_bg, c_cg, c_hb), l, conv_mix_w[l])
        o, s_new = _delta_prompt(qk, v, gcb, bcb, gout.reshape(b, seq, val_w), og_row, pad, n_heads, dk)
        merged = _merge(o.reshape(m, val_w), y_b.reshape(m, w_b), h, wua, wub, w_rest, c_za, c_zb, l, tm)
        xp = _mm_res(merged, wo, l, xp, tm, tn_o, "out_proj")
        h2 = _rmsnorm(xp, norm2_g[l][None], BF16, rows_norm)
        act, nb_f = _ffn_prompt(h2.reshape(b, seq, d), w_ffn_up, l, conv_ffn_w[l], cb_row)
        xp = _mm_res(act.reshape(m, d_ff), wfd, l, xp, tm, tn_f, "ffn_down")
        sp.append(s_new); qp.append(nb_q); mp.append(nb_m); fp.append(nb_f)

        hs = _rmsnorm(xs, norm1_g[l][None], BF16, bs)
        qkv_s, z_new = _qkv_sample(hs, w_t, l, conv_qkv_w[l], st_qkv, dk, key_w)
        gcb_s, bcb_s = _gates_sample(hs, w_t, o_ab, l, alog_row, dtb_row, n_heads, dk)
        gout_s = _mm(hs, w_rest, l, c_go, val_w, bs, 1024, "gout")
        yb_s, pre_new = _mixb_sample(hs, w_rest, (c_bg, c_cg, c_hb), l, conv_mix_w[l], st_mix)
        q3 = qkv_s[:, :key_w].reshape(bs, n_heads, dk)
        k3 = qkv_s[:, key_w:2 * key_w].reshape(bs, n_heads, dk)
        v3 = qkv_s[:, 2 * key_w:].reshape(bs, n_heads, dv)
        o_s, s_sample = _delta_sample(q3, k3, v3, gcb_s, bcb_s, gout_s.reshape(bs, n_heads, dv), og_row,
                                      state_delta, l, s_sample)
        merged_s = _merge(o_s.reshape(bs, val_w), yb_s, hs, wua, wub, w_rest, c_za, c_zb, l, bs)
        xs = _mm_res(merged_s, wo, l, xs, bs, tn_o, "out_proj")
        h2s = _rmsnorm(xs, norm2_g[l][None], BF16, bs)
        act_s, gate_new = _ffn_sample(h2s, w_ffn_up, l, conv_ffn_w[l], cb_row, st_ffn)
        xs = _mm_res(act_s, wfd, l, xs, bs, tn_f, "ffn_down")
        qs.append(_roll_state(state_conv_qkv[l], z_new))
        ms.append(_roll_state(state_conv_mix[l], pre_new))
        fs.append(_roll_state(state_conv_ffn[l], gate_new))

    y_prompt = _rmsnorm(xp, final_g[None], F32, rows_norm // 2).reshape(b, seq, d)[:, n_meta:]
    y_sample = _rmsnorm(xs, final_g[None], F32, bs).reshape(bs, 1, d)
    return (y_prompt, y_sample, jnp.stack(sp), s_sample, jnp.stack(qp), jnp.stack(qs),
            jnp.stack(mp), jnp.stack(ms), jnp.stack(fp), jnp.stack(fs))
```

```python
import functools

import jax
import jax.numpy as jnp
from jax import lax
from jax.experimental import pallas as pl
from jax.experimental.pallas import tpu as pltpu

F32 = jnp.float32
BF16 = jnp.bfloat16
EPS = 1e-6
HIGHEST = lax.Precision.HIGHEST
CHUNK = 64
LANES = 128
SUBLANES = 8
MXU_COLS = 256


def _sigmoid(x):
    return 1.0 / (1.0 + jnp.exp(-x))


def _silu(x):
    return x * _sigmoid(x)


def _softplus(x):
    return jnp.maximum(x, 0.0) + jnp.log1p(jnp.exp(-jnp.abs(x)))


def _dot(a, b, precision=None):
    return jnp.dot(a, b, preferred_element_type=F32, precision=precision)


def _dot_nt(a, wt):
    return lax.dot_general(a, wt.astype(BF16), (((1,), (1,)), ((), ())), preferred_element_type=F32)


def _call(kernel, *, grid, in_specs, out_specs, out_shape, scratch_shapes=(), name=None,
          input_output_aliases=None):
    return pl.pallas_call(
        kernel, grid=grid, in_specs=in_specs, out_specs=out_specs, out_shape=out_shape,
        scratch_shapes=list(scratch_shapes), name=name,
        input_output_aliases=input_output_aliases or {},
        compiler_params=pltpu.CompilerParams(dimension_semantics=("arbitrary",) * len(grid)))


def _rmsnorm_kernel(x_ref, g_ref, o_ref):
    x = x_ref[...]
    ms = jnp.mean(x * x, axis=-1, keepdims=True)
    o_ref[...] = ((x * lax.rsqrt(ms + EPS)) * g_ref[...]).astype(o_ref.dtype)


def _rmsnorm(x2d, g_row, out_dtype, rows):
    m, d = x2d.shape
    return _call(
        _rmsnorm_kernel, grid=(m // rows,),
        in_specs=[pl.BlockSpec((rows, d), lambda i: (i, 0)), pl.BlockSpec((1, d), lambda i: (0, 0))],
        out_specs=pl.BlockSpec((rows, d), lambda i: (i, 0)),
        out_shape=jax.ShapeDtypeStruct((m, d), out_dtype), name="rmsnorm")(x2d, g_row)


def _mm_kernel(a_ref, w_ref, o_ref):
    o_ref[...] = _dot_nt(a_ref[...], w_ref[...]).astype(o_ref.dtype)


def _mm(a, wt_stack, layer, row0, n, tm, tn, name):
    m, k = a.shape
    r0 = row0 // tn
    return _call(
        _mm_kernel, grid=(m // tm, n // tn),
        in_specs=[pl.BlockSpec((tm, k), lambda i, j: (i, 0)),
                  pl.BlockSpec((None, tn, k), lambda i, j: (layer, j + r0, 0))],
        out_specs=pl.BlockSpec((tm, tn), lambda i, j: (i, j)),
        out_shape=jax.ShapeDtypeStruct((m, n), F32), name=name)(a, wt_stack)


def _mm_res_kernel(a_ref, w_ref, r_ref, o_ref):
    o_ref[...] = r_ref[...] + _dot(a_ref[...], w_ref[...].astype(BF16))


def _mm_res(a, w_stack, layer, res, tm, tn, name):
    m, k = a.shape
    n = w_stack.shape[-1]
    return _call(
        _mm_res_kernel, grid=(m // tm, n // tn),
        in_specs=[pl.BlockSpec((tm, k), lambda i, j: (i, 0)),
                  pl.BlockSpec((None, k, tn), lambda i, j: (layer, 0, j)),
                  pl.BlockSpec((tm, tn), lambda i, j: (i, j))],
        out_specs=pl.BlockSpec((tm, tn), lambda i, j: (i, j)),
        out_shape=jax.ShapeDtypeStruct((m, n), F32), name=name)(a, w_stack, res)


def _l2norm_heads(x, dk, scale):
    parts = []
    for i in range(x.shape[1] // dk):
        p = x[:, i * dk:(i + 1) * dk]
        p = p * lax.rsqrt(jnp.sum(p * p, axis=-1, keepdims=True) + EPS)
        parts.append(p * scale if scale != 1.0 else p)
    return jnp.concatenate(parts, axis=-1)


def _qkv_finish(c, o_write, n, n_q, n_k, dk):
    if n_q == 0 and n_k == 0:
        o_write(c)
        return

    @pl.when(n < n_q)
    def _():
        o_write(_l2norm_heads(c, dk, dk ** -0.5))

    @pl.when(jnp.logical_and(n >= n_q, n < n_q + n_k))
    def _():
        o_write(_l2norm_heads(c, dk, 1.0))

    @pl.when(n >= n_q + n_k)
    def _():
        o_write(c)


def _conv_rows(zp_ref, cw_ref, seq, taps, cols=slice(None)):
    acc = zp_ref[SUBLANES:SUBLANES + seq, :] * cw_ref[taps - 1:taps, cols]
    for i in range(taps - 1):
        s = taps - 1 - i
        acc = acc + zp_ref[SUBLANES - s:SUBLANES - s + seq, :] * cw_ref[i:i + 1, cols]
    return acc


TILES_PER_STEP = 2


def _qkv_prompt_kernel(h_ref, w_ref, cw_ref, qkv_ref, nb_ref, *zs_refs, seq, pad, taps, q_steps, l2, dk):
    j = pl.program_id(1)
    tn = MXU_COLS
    h = h_ref[...]
    qkv_ref[0:pad, :] = jnp.zeros((pad, qkv_ref.shape[1]), qkv_ref.dtype)
    scale = jnp.where(j < q_steps, dk ** -0.5, 1.0)
    for t, zs_ref in enumerate(zs_refs):
        cols = slice(t * tn, (t + 1) * tn)
        zs_ref[0:SUBLANES, :] = jnp.zeros((SUBLANES, tn), F32)
        zs_ref[SUBLANES:SUBLANES + seq, :] = _dot_nt(h, w_ref[cols, :])
        nb_ref[:, cols] = zs_ref[SUBLANES + seq - (taps - 1):SUBLANES + seq, :]
        c = _silu(_conv_rows(zs_ref, cw_ref, seq, taps, cols))
        if l2:
            c = _l2norm_heads(c, dk, 1.0) * scale
        qkv_ref[pad:pad + seq, cols] = c.astype(qkv_ref.dtype)


def _qkv_prompt(h3, w_t, layer, cw, pad, dk, col0, width, n_q, l2, out_dtype):
    b, seq, d = h3.shape
    taps = cw.shape[0]
    tn = MXU_COLS
    tw = TILES_PER_STEP * tn
    assert col0 % tw == 0 and width % tw == 0 and n_q % TILES_PER_STEP == 0
    c0 = col0 // tw
    kern = functools.partial(_qkv_prompt_kernel, seq=seq, pad=pad, taps=taps, q_steps=n_q // TILES_PER_STEP,
                             l2=l2, dk=dk)
    return _call(
        kern, grid=(b, width // tw),
        in_specs=[pl.BlockSpec((None, seq, d), lambda i, j: (i, 0, 0), pipeline_mode=pl.Buffered(1)),
                  pl.BlockSpec((None, tw, d), lambda i, j: (layer, j + c0, 0)),
                  pl.BlockSpec((taps, tw), lambda i, j: (0, j + c0))],
        out_specs=[pl.BlockSpec((None, seq + pad, tw), lambda i, j: (i, 0, j)),
                   pl.BlockSpec((None, taps - 1, tw), lambda i, j: (i, 0, j))],
        out_shape=[jax.ShapeDtypeStruct((b, seq + pad, width), out_dtype),
                   jax.ShapeDtypeStruct((b, taps - 1, width), F32)],
        scratch_shapes=[pltpu.VMEM((seq + SUBLANES, tn), F32)] * TILES_PER_STEP,
        name="qkv_prompt")(h3, w_t, cw)


def _qkv_sample_kernel(h_ref, w_ref, cw_ref, st_ref, qkv_ref, z_ref, *, taps, n_q, n_k, dk):
    n = pl.program_id(0)
    z = _dot_nt(h_ref[...], w_ref[...])
    z_ref[...] = z
    acc = z * cw_ref[taps - 1:taps, :]
    for i in range(taps - 1):
        acc = acc + st_ref[i] * cw_ref[i:i + 1, :]
    c = _silu(acc)

    def write(val):
        qkv_ref[...] = val

    _qkv_finish(c, write, n, n_q, n_k, dk)


def _qkv_sample(h2, w_stack, layer, cw, st_stack, dk, key_w):
    bs, d = h2.shape
    taps, width = cw.shape
    tn = MXU_COLS
    kern = functools.partial(_qkv_sample_kernel, taps=taps, n_q=key_w // tn, n_k=key_w // tn, dk=dk)
    return _call(
        kern, grid=(width // tn,),
        in_specs=[pl.BlockSpec((bs, d), lambda j: (0, 0)),
                  pl.BlockSpec((None, tn, d), lambda j: (layer, j, 0)),
                  pl.BlockSpec((taps, tn), lambda j: (0, j)),
                  pl.BlockSpec((None, taps - 1, bs, tn), lambda j: (layer, 0, 0, j))],
        out_specs=[pl.BlockSpec((bs, tn), lambda j: (0, j)), pl.BlockSpec((bs, tn), lambda j: (0, j))],
        out_shape=[jax.ShapeDtypeStruct((bs, width), F32), jax.ShapeDtypeStruct((bs, width), F32)],
        name="qkv_sample")(h2, w_stack, cw, st_stack)


def _gates_math(ab, alog_ref, dtb_ref):
    g = -jnp.exp(alog_ref[...]) * _softplus(ab + dtb_ref[...])
    return g, _sigmoid(ab)


def _expand_heads(x, first_head, lane_off, heads, dk):
    src = lax.broadcasted_iota(jnp.int32, (LANES, heads * dk), 0)
    dst = lax.broadcasted_iota(jnp.int32, (LANES, heads * dk), 1) // dk + (first_head + lane_off)
    onehot = jnp.where(src == dst, 1.0, 0.0).astype(BF16)
    out = None
    for _ in range(3):
        piece = x.astype(BF16)
        x = x - piece.astype(F32)
        term = _dot(piece, onehot)
        out = term if out is None else out + term
    return out


def _gates_prompt_kernel(h_ref, wab_ref, alog_ref, dtb_ref, g_ref, b_ref, gp_ref, bp_ref,
                         *, seq, pad, n_heads, dk, heads_per_step):
    j = pl.program_id(1)

    @pl.when(j == 0)
    def _():
        ab = _dot_nt(h_ref[...], wab_ref[...])
        g, beta = _gates_math(ab, alog_ref, dtb_ref)
        zeros = jnp.zeros((pad, LANES), F32)
        gp_ref[0:pad, :] = zeros
        bp_ref[0:pad, :] = zeros
        gp_ref[pad:pad + seq, :] = g
        bp_ref[pad:pad + seq, :] = beta
        row = lax.broadcasted_iota(jnp.int32, (CHUNK, CHUNK), 0)
        col = lax.broadcasted_iota(jnp.int32, (CHUNK, CHUNK), 1)
        tril = (row >= col).astype(F32)
        for c in range((seq + pad) // CHUNK):
            rows = slice(c * CHUNK, (c + 1) * CHUNK)
            gp_ref[rows, :] = _dot(tril, gp_ref[rows, :], precision=HIGHEST)

    g_ref[...] = _expand_heads(gp_ref[...], j * heads_per_step, 0, heads_per_step, dk)
    b_ref[...] = _expand_heads(bp_ref[...], j * heads_per_step, n_heads, heads_per_step, dk)


def _gates_prompt(h3, w_t, ab_row0, layer, alog_row, dtb_row, pad, n_heads, dk):
    assert ab_row0 % LANES == 0
    b, seq, d = h3.shape
    t = seq + pad
    hps = 4
    kern = functools.partial(_gates_prompt_kernel, seq=seq, pad=pad, n_heads=n_heads, dk=dk, heads_per_step=hps)
    out = jax.ShapeDtypeStruct((b, t, n_heads * dk), F32)
    return _call(
        kern, grid=(b, n_heads // hps),
        in_specs=[pl.BlockSpec((None, seq, d), lambda i, j: (i, 0, 0), pipeline_mode=pl.Buffered(1)),
                  pl.BlockSpec((None, LANES, d), lambda i, j: (layer, ab_row0 // LANES, 0)),
                  pl.BlockSpec((1, LANES), lambda i, j: (0, 0)),
                  pl.BlockSpec((1, LANES), lambda i, j: (0, 0))],
        out_specs=[pl.BlockSpec((None, t, hps * dk), lambda i, j: (i, 0, j)),
                   pl.BlockSpec((None, t, hps * dk), lambda i, j: (i, 0, j))],
        out_shape=[out, out],
        scratch_shapes=[pltpu.VMEM((t, LANES), F32), pltpu.VMEM((t, LANES), F32)],
        name="gates_prompt")(h3, w_t, alog_row, dtb_row)


def _gates_sample_kernel(h_ref, wab_ref, alog_ref, dtb_ref, g_ref, b_ref, *, n_heads, dk):
    ab = _dot_nt(h_ref[...], wab_ref[...])
    g, beta = _gates_math(ab, alog_ref, dtb_ref)
    g_ref[...] = _expand_heads(g, 0, 0, n_heads, dk)
    b_ref[...] = _expand_heads(beta, 0, n_heads, n_heads, dk)


def _gates_sample(h2, w_t, ab_row0, layer, alog_row, dtb_row, n_heads, dk):
    bs, d = h2.shape
    out = jax.ShapeDtypeStruct((bs, n_heads * dk), F32)
    return _call(
        functools.partial(_gates_sample_kernel, n_heads=n_heads, dk=dk), grid=(1,),
        in_specs=[pl.BlockSpec((bs, d), lambda i: (0, 0)),
                  pl.BlockSpec((None, LANES, d), lambda i: (layer, ab_row0 // LANES, 0)),
                  pl.BlockSpec((1, LANES), lambda i: (0, 0)),
                  pl.BlockSpec((1, LANES), lambda i: (0, 0))],
        out_specs=[pl.BlockSpec((bs, n_heads * dk), lambda i: (0, 0)),
                   pl.BlockSpec((bs, n_heads * dk), lambda i: (0, 0))],
        out_shape=[out, out], name="gates_sample")(h2, w_t, alog_row, dtb_row)


def _mixb_prompt_kernel(h_ref, wb_ref, wc_ref, wh_ref, cw_ref, y_ref, nb_ref, ps_ref, *, seq, taps):
    h = h_ref[...]
    pre = _dot_nt(h, wc_ref[...]) * _dot_nt(h, wh_ref[...])
    ps_ref[0:SUBLANES, :] = jnp.zeros((SUBLANES, pre.shape[1]), F32)
    ps_ref[SUBLANES:SUBLANES + seq, :] = pre
    nb_ref[...] = ps_ref[SUBLANES + seq - (taps - 1):SUBLANES + seq, :]
    acc = pre * cw_ref[taps - 1:taps, :]
    for i in range(taps - 1):
        s = taps - 1 - i
        acc = acc + ps_ref[SUBLANES - s:SUBLANES - s + seq, :] * cw_ref[i:i + 1, :]
    y_ref[...] = (_dot_nt(h, wb_ref[...]) * acc).astype(y_ref.dtype)


def _mixb_prompt(h3, w_stack, cols, layer, cw):
    b, seq, d = h3.shape
    taps, width = cw.shape
    tn = MXU_COLS

    def wspec(col0):
        return pl.BlockSpec((None, tn, d), lambda i, j: (layer, j + col0 // tn, 0))

    return _call(
        functools.partial(_mixb_prompt_kernel, seq=seq, taps=taps), grid=(b, width // tn),
        in_specs=[pl.BlockSpec((None, seq, d), lambda i, j: (i, 0, 0), pipeline_mode=pl.Buffered(1)),
                  wspec(cols[0]), wspec(cols[1]), wspec(cols[2]), pl.BlockSpec((taps, tn), lambda i, j: (0, j))],
        out_specs=[pl.BlockSpec((None, seq, tn), lambda i, j: (i, 0, j)),
                   pl.BlockSpec((None, taps - 1, tn), lambda i, j: (i, 0, j))],
        out_shape=[jax.ShapeDtypeStruct((b, seq, width), BF16),
                   jax.ShapeDtypeStruct((b, taps - 1, width), F32)],
        scratch_shapes=[pltpu.VMEM((seq + SUBLANES, tn), F32)],
        name="mixb_prompt")(h3, w_stack, w_stack, w_stack, cw)


def _mixb_sample_kernel(h_ref, wb_ref, wc_ref, wh_ref, cw_ref, st_ref, y_ref, pre_ref, *, taps):
    h = h_ref[...]
    pre = _dot_nt(h, wc_ref[...]) * _dot_nt(h, wh_ref[...])
    pre_ref[...] = pre
    acc = pre * cw_ref[taps - 1:taps, :]
    for i in range(taps - 1):
        acc = acc + st_ref[i] * cw_ref[i:i + 1, :]
    y_ref[...] = (_dot_nt(h, wb_ref[...]) * acc).astype(y_ref.dtype)


def _mixb_sample(h2, w_stack, cols, layer, cw, st_stack):
    bs, d = h2.shape
    taps, width = cw.shape
    tn = MXU_COLS

    def wspec(col0):
        return pl.BlockSpec((None, tn, d), lambda j: (layer, j + col0 // tn, 0))

    return _call(
        functools.partial(_mixb_sample_kernel, taps=taps), grid=(width // tn,),
        in_specs=[pl.BlockSpec((bs, d), lambda j: (0, 0)), wspec(cols[0]), wspec(cols[1]), wspec(cols[2]),
                  pl.BlockSpec((taps, tn), lambda j: (0, j)),
                  pl.BlockSpec((None, taps - 1, bs, tn), lambda j: (layer, 0, 0, j))],
        out_specs=[pl.BlockSpec((bs, tn), lambda j: (0, j)), pl.BlockSpec((bs, tn), lambda j: (0, j))],
        out_shape=[jax.ShapeDtypeStruct((bs, width), BF16), jax.ShapeDtypeStruct((bs, width), F32)],
        name="mixb_sample")(h2, w_stack, w_stack, w_stack, cw, st_stack)


def _bmm(a, b):
    return jnp.einsum("cij,cjk->cik", a.astype(BF16), b.astype(BF16), preferred_element_type=F32)


def _bmm_nt(a, b):
    return jnp.einsum("cid,cjd->cij", a.astype(BF16), b.astype(BF16), preferred_element_type=F32)


def _unit_lower_inverse(lm, row, col, eye):
    blk16 = (row // 16) == (col // 16)
    blk32 = (row // 32) == (col // 32)
    ld = jnp.where(blk16, lm, 0.0)
    lo1 = jnp.where(jnp.logical_and(blk32, jnp.logical_not(blk16)), lm, 0.0)
    lo2 = jnp.where(blk32, 0.0, lm)
    x = eye.astype(F32) - ld
    m = _bmm(ld, ld)
    x = x + _bmm(x, m)
    m = _bmm(m, m)
    x = x + _bmm(x, m)
    m = _bmm(m, m)
    x = x + _bmm(x, m)
    x = x - _bmm(x, _bmm(lo1, x))
    x = x - _bmm(x, _bmm(lo2, x))
    return x


HEADS_PER_STEP = 2


def _delta_prompt_kernel(q_ref, k_ref, v_ref, g_ref, b_ref, go_ref, og_ref, o_ref, s_ref,
                         w_s, u_s, kt_s, qk_s, oo_s, *, seq, pad, dk):
    t = seq + pad
    nc = t // CHUNK
    row = lax.broadcasted_iota(jnp.int32, (CHUNK, CHUNK), 0)
    col = lax.broadcasted_iota(jnp.int32, (CHUNK, CHUNK), 1)
    eye = row == col
    incl = row >= col
    for hh in range(HEADS_PER_STEP):
        lanes = slice(hh * dk, (hh + 1) * dk)
        q3 = q_ref[:, lanes].reshape(nc, CHUNK, dk)
        k3 = k_ref[:, lanes].reshape(nc, CHUNK, dk)
        v3 = v_ref[:, lanes].reshape(nc, CHUNK, dk)
        g3 = g_ref[:, lanes].reshape(nc, CHUNK, dk)
        b3 = b_ref[:, lanes].reshape(nc, CHUNK, dk)
        kf = k3.astype(F32)
        gi = g3[:, :, :CHUNK]
        gj = jnp.sum(jnp.where(eye, gi, 0.0), axis=1, keepdims=True)
        decay = jnp.where(incl, jnp.exp(jnp.where(incl, gi - gj, 0.0)), 0.0)
        lm = jnp.where(row > col, b3[:, :, :CHUNK] * _bmm_nt(k3, k3) * decay, 0.0)
        tinv = _unit_lower_inverse(lm, row, col, eye)
        u_s[hh] = _bmm(tinv, b3 * v3).reshape(t, dk)
        w_s[hh] = _bmm(tinv, b3 * jnp.exp(g3) * kf).reshape(t, dk).astype(BF16)
        qk_s[hh] = (_bmm_nt(q3, k3) * decay).reshape(t, CHUNK).astype(BF16)
        kt_s[hh] = (kf * jnp.exp(g3[:, CHUNK - 1:CHUNK, :] - g3)).reshape(t, dk).astype(BF16)

    def body(c, states):
        r0 = pl.multiple_of(c * CHUNK, CHUNK)
        rows = pl.ds(r0, CHUNK)
        new_states = []
        for hh in range(HEADS_PER_STEP):
            lanes = slice(hh * dk, (hh + 1) * dk)
            s = states[hh]
            sb = s.astype(BF16)
            u = u_s[hh, rows, :] - _dot(w_s[hh, rows, :], sb)
            ub = u.astype(BF16)
            oo_s[hh, rows, :] = (jnp.exp(g_ref[rows, lanes]) * _dot(q_ref[rows, lanes], sb)
                                 + _dot(qk_s[hh, rows, :], ub))
            g_tail = g_ref[pl.ds(pl.multiple_of(r0 + CHUNK - SUBLANES, SUBLANES), SUBLANES), lanes]
            s_scale = jnp.exp(g_tail[SUBLANES - 1:SUBLANES, :])
            new_states.append(s_scale * s + lax.dot_general(kt_s[hh, rows, :], ub, (((0,), (0,)), ((), ())),
                                                            preferred_element_type=F32))
        return tuple(new_states)

    init = tuple(jnp.zeros((dk, dk), F32) for _ in range(HEADS_PER_STEP))
    states = lax.fori_loop(0, nc, body, init)
    for hh in range(HEADS_PER_STEP):
        lanes = slice(hh * dk, (hh + 1) * dk)
        s_ref[hh] = states[hh]
        o = oo_s[hh, pad:pad + seq, :]
        o = (o * lax.rsqrt(jnp.mean(o * o, axis=-1, keepdims=True) + EPS)) * og_ref[...]
        o_ref[:, lanes] = (o * _silu(go_ref[:, lanes])).astype(o_ref.dtype)


def _delta_prompt(qk, v, gcb, bcb, gout3, og_row, pad, n_heads, dk):
    b, t, _ = v.shape
    seq = t - pad
    hps = HEADS_PER_STEP
    wd = hps * dk

    def col_spec(rows, off):
        return pl.BlockSpec((None, rows, wd), lambda i, j: (i, 0, j + off))

    kern = functools.partial(_delta_prompt_kernel, seq=seq, pad=pad, dk=dk)
    return _call(
        kern, grid=(b, n_heads // hps),
        in_specs=[col_spec(t, 0), col_spec(t, n_heads // hps), col_spec(t, 0),
                  col_spec(t, 0), col_spec(t, 0), col_spec(seq, 0),
                  pl.BlockSpec((1, dk), lambda i, j: (0, 0))],
        out_specs=[col_spec(seq, 0), pl.BlockSpec((None, hps, dk, dk), lambda i, j: (i, j, 0, 0))],
        out_shape=[jax.ShapeDtypeStruct((b, seq, n_heads * dk), BF16),
                   jax.ShapeDtypeStruct((b, n_heads, dk, dk), F32)],
        scratch_shapes=[pltpu.VMEM((hps, t, dk), BF16), pltpu.VMEM((hps, t, dk), F32),
                        pltpu.VMEM((hps, t, dk), BF16), pltpu.VMEM((hps, t, CHUNK), BF16),
                        pltpu.VMEM((hps, t, dk), F32)],
        name="delta_prompt")(qk, qk, v, gcb, bcb, gout3, og_row)


def _delta_sample_kernel(q_ref, k_ref, v_ref, g_ref, b_ref, go_ref, og_ref, s_ref, *rest, n_heads, dk):
    o_ref, sn_ref = rest[-2:]
    for bi in range(SAMPLES_PER_STEP):
        q = q_ref[bi]
        k = k_ref[bi]
        v = v_ref[bi]
        kq = jnp.concatenate([k, q, jnp.zeros((LANES - 2 * n_heads, dk), F32)], axis=0)
        kq_t = kq.T
        qk = jnp.sum(q * k, axis=-1, keepdims=True)
        outs = []
        for h in range(n_heads):
            s = s_ref[bi, h]
            a = jnp.exp(g_ref[bi, :, h * dk:(h + 1) * dk])
            beta = b_ref[bi, :, h * dk:(h + 1) * dk]
            kcol = kq_t[:, h:h + 1]
            qcol = kq_t[:, n_heads + h:n_heads + h + 1]
            ks = jnp.sum(kcol * s, axis=0, keepdims=True)
            qs = jnp.sum(qcol * s, axis=0, keepdims=True)
            u = beta * (v[h:h + 1, :] - a * ks)
            outs.append(a * qs + qk[h:h + 1, :] * u)
            sn_ref[bi, h] = a * s + kcol * u
        o = jnp.concatenate(outs, axis=0)
        o = (o * lax.rsqrt(jnp.mean(o * o, axis=-1, keepdims=True) + EPS)) * og_ref[...]
        o_ref[bi] = (o * _silu(go_ref[bi])).astype(o_ref.dtype)


SAMPLES_PER_STEP = 2


def _delta_sample(q3, k3, v3, gcb, bcb, gout3, og_row, state, layer, prev_out):
    bs, n_heads, dk = q3.shape
    depth = state.shape[0]
    nb = SAMPLES_PER_STEP
    hspec = pl.BlockSpec((nb, n_heads, dk), lambda i: (i, 0, 0))
    rspec = pl.BlockSpec((nb, 1, n_heads * dk), lambda i: (i, 0, 0))
    sspec = pl.BlockSpec((None, nb, n_heads, dk, dk), lambda i: (layer, i, 0, 0, 0))
    in_specs = [hspec, hspec, hspec, rspec, rspec, hspec, pl.BlockSpec((1, dk), lambda i: (0, 0)), sspec]
    args = [q3, k3, v3, gcb.reshape(bs, 1, -1), bcb.reshape(bs, 1, -1), gout3, og_row, state]
    aliases = {}
    if prev_out is not None:
        aliases = {len(args): 1}
        in_specs.append(pl.BlockSpec(memory_space=pl.ANY))
        args.append(prev_out)
    return _call(
        functools.partial(_delta_sample_kernel, n_heads=n_heads, dk=dk), grid=(bs // nb,),
        in_specs=in_specs, out_specs=[hspec, sspec],
        out_shape=[jax.ShapeDtypeStruct((bs, n_heads, dk), BF16),
                   jax.ShapeDtypeStruct((depth, bs, n_heads, dk, dk), F32)],
        input_output_aliases=aliases, name="delta_sample")(*args)


def _merge_kernel(o_ref, y_ref, h_ref, wa_ref, wb_ref, wza_ref, wzb_ref, m_ref):
    h = h_ref[...]
    za = _sigmoid(_dot_nt(h, wza_ref[...]))
    zb = _sigmoid(_dot_nt(h, wzb_ref[...]))
    pa = _dot(o_ref[...], wa_ref[...].astype(BF16))
    pb = _dot(y_ref[...], wb_ref[...].astype(BF16))
    m_ref[...] = (za * pa + zb * pb).astype(m_ref.dtype)


def _merge(o2, y2, h2, wa_stack, wb_stack, wz_stack, col_za, col_zb, layer, tm):
    m, d = h2.shape
    ka = o2.shape[1]
    kb = y2.shape[1]
    n = wa_stack.shape[-1]
    tn = MXU_COLS

    def wspec(kk):
        return pl.BlockSpec((None, kk, tn), lambda i, j: (layer, 0, j))

    def wtspec(row0):
        return pl.BlockSpec((None, tn, d), lambda i, j: (layer, j + row0 // tn, 0))

    return _call(
        _merge_kernel, grid=(m // tm, n // tn),
        in_specs=[pl.BlockSpec((tm, ka), lambda i, j: (i, 0)), pl.BlockSpec((tm, kb), lambda i, j: (i, 0)),
                  pl.BlockSpec((tm, d), lambda i, j: (i, 0)), wspec(ka), wspec(kb),
                  wtspec(col_za), wtspec(col_zb)],
        out_specs=pl.BlockSpec((tm, tn), lambda i, j: (i, j)),
        out_shape=jax.ShapeDtypeStruct((m, n), BF16), name="merge")(
            o2, y2, h2, wa_stack, wb_stack, wz_stack, wz_stack)


def _ffn_prompt_kernel(h_ref, wg_ref, wv_ref, cw_ref, cb_ref, a_ref, nb_ref, gs_ref, *, seq, taps):
    h = h_ref[...]
    gs_ref[0:SUBLANES, :] = jnp.zeros((SUBLANES, gs_ref.shape[1]), F32)
    gs_ref[SUBLANES:SUBLANES + seq, :] = _dot(h, wg_ref[...].astype(BF16))
    nb_ref[...] = gs_ref[SUBLANES + seq - (taps - 1):SUBLANES + seq, :]
    acc = _conv_rows(gs_ref, cw_ref, seq, taps)
    a_ref[...] = (_silu(acc + cb_ref[...]) * _dot(h, wv_ref[...].astype(BF16))).astype(a_ref.dtype)


def _ffn_prompt(h3, wup_stack, layer, cw, cb_row):
    b, seq, d = h3.shape
    dff = cw.shape[1]
    taps = cw.shape[0]
    tn = MXU_COLS
    nt = dff // tn
    return _call(
        functools.partial(_ffn_prompt_kernel, seq=seq, taps=taps), grid=(b, nt),
        in_specs=[pl.BlockSpec((None, seq, d), lambda i, j: (i, 0, 0), pipeline_mode=pl.Buffered(1)),
                  pl.BlockSpec((None, d, tn), lambda i, j: (layer, 0, j)),
                  pl.BlockSpec((None, d, tn), lambda i, j: (layer, 0, j + nt)),
                  pl.BlockSpec((taps, tn), lambda i, j: (0, j)),
                  pl.BlockSpec((1, tn), lambda i, j: (0, j))],
        out_specs=[pl.BlockSpec((None, seq, tn), lambda i, j: (i, 0, j)),
                   pl.BlockSpec((None, taps - 1, tn), lambda i, j: (i, 0, j))],
        out_shape=[jax.ShapeDtypeStruct((b, seq, dff), BF16),
                   jax.ShapeDtypeStruct((b, taps - 1, dff), F32)],
        scratch_shapes=[pltpu.VMEM((seq + SUBLANES, tn), F32)],
        name="ffn_prompt")(h3, wup_stack, wup_stack, cw, cb_row)


def _ffn_sample_kernel(h_ref, wg_ref, wv_ref, cw_ref, cb_ref, st_ref, a_ref, g_ref, *, taps):
    h = h_ref[...]
    gate = _dot(h, wg_ref[...].astype(BF16))
    g_ref[...] = gate
    acc = gate * cw_ref[taps - 1:taps, :]
    for i in range(taps - 1):
        acc = acc + st_ref[i] * cw_ref[i:i + 1, :]
    a_ref[...] = (_silu(acc + cb_ref[...]) * _dot(h, wv_ref[...].astype(BF16))).astype(a_ref.dtype)


def _ffn_sample(h2, wup_stack, layer, cw, cb_row, st_stack):
    bs, d = h2.shape
    dff = cw.shape[1]
    taps = cw.shape[0]
    tn = MXU_COLS
    nt = dff // tn
    return _call(
        functools.partial(_ffn_sample_kernel, taps=taps), grid=(nt,),
        in_specs=[pl.BlockSpec((bs, d), lambda j: (0, 0)),
                  pl.BlockSpec((None, d, tn), lambda j: (layer, 0, j)),
                  pl.BlockSpec((None, d, tn), lambda j: (layer, 0, j + nt)),
                  pl.BlockSpec((taps, tn), lambda j: (0, j)),
                  pl.BlockSpec((1, tn), lambda j: (0, j)),
                  pl.BlockSpec((None, taps - 1, bs, tn), lambda j: (layer, 0, 0, j))],
        out_specs=[pl.BlockSpec((bs, tn), lambda j: (0, j)), pl.BlockSpec((bs, tn), lambda j: (0, j))],
        out_shape=[jax.ShapeDtypeStruct((bs, dff), BF16), jax.ShapeDtypeStruct((bs, dff), F32)],
        name="ffn_sample")(h2, wup_stack, wup_stack, cw, cb_row, st_stack)


def _cast_kernel(x_ref, o_ref):
    o_ref[...] = x_ref[0].astype(o_ref.dtype)


def _cast_rows_bf16(w_t, row0, rows):
    depth, _, k = w_t.shape
    tr = 512
    assert row0 % 16 == 0 and rows % tr == 0
    return _call(
        _cast_kernel, grid=(depth, rows // tr),
        in_specs=[pl.BlockSpec((pl.Element(1), pl.Element(tr), pl.Element(k)),
                               lambda l, i: (l, pl.multiple_of(row0 + i * tr, 16), 0))],
        out_specs=pl.BlockSpec((None, tr, k), lambda l, i: (l, i, 0)),
        out_shape=jax.ShapeDtypeStruct((depth, rows, k), BF16), name="cast_rows")(w_t)


def _pad_lanes(x2d):
    return jnp.pad(x2d, ((0, 0), (0, LANES - x2d.shape[1])))


def _roll_state(state_l, new_row):
    return jnp.concatenate([state_l[:, 1:], new_row[:, None]], axis=1)


def kernel(x_prompt, x_sample, state_delta, state_conv_qkv, state_conv_mix, state_conv_ffn, meta_tokens,
           norm1_g, w_in, conv_qkv_w, a_log, dt_bias, onorm_g, conv_mix_w, w_up_a, w_up_b, w_o, norm2_g,
           w_ffn_up, conv_ffn_w, conv_ffn_b, w_ffn_down, final_g):
    b, seq0, d = x_prompt.shape
    bs = x_sample.shape[0]
    depth, _, n_heads, dk, dv = state_delta.shape
    assert dk == dv == LANES and x_sample.shape[1] == 1
    n_meta = meta_tokens.shape[0]
    seq = seq0 + n_meta
    pad = (-seq) % CHUNK
    key_w = n_heads * dk
    val_w = n_heads * dv
    qkv_w = state_conv_qkv.shape[-1]
    w_b = state_conv_mix.shape[-1]
    d_ff = state_conv_ffn.shape[-1]
    assert 2 * n_heads <= LANES

    o_ab = qkv_w
    o_go = o_ab + 2 * n_heads
    o_bg = o_go + val_w
    o_cg = o_bg + w_b
    o_hb = o_cg + w_b
    o_za = o_hb + w_b
    o_zb = o_za + d
    assert o_zb + d == w_in.shape[-1]

    w_t = jnp.swapaxes(w_in, 1, 2)
    w_rest = _cast_rows_bf16(w_t, o_go, w_in.shape[-1] - o_go)
    c_go, c_bg, c_cg, c_hb, c_za, c_zb = (o - o_go for o in (o_go, o_bg, o_cg, o_hb, o_za, o_zb))
    wua, wub = w_up_a, w_up_b
    wo, wfd = w_o.astype(BF16), w_ffn_down.astype(BF16)
    st_qkv = jnp.swapaxes(state_conv_qkv, 1, 2)
    st_mix = jnp.swapaxes(state_conv_mix, 1, 2)
    st_ffn = jnp.swapaxes(state_conv_ffn, 1, 2)

    xp = jnp.concatenate([jnp.broadcast_to(meta_tokens[None], (b, n_meta, d)), x_prompt], axis=1)
    xp = xp.reshape(b * seq, d)
    xs = x_sample.reshape(bs, d)
    m = b * seq
    tm = seq // 3
    rows_norm = tm
    assert seq % 3 == 0 and tm % 16 == 0
    tn_o = 512
    tn_f = MXU_COLS

    sp, qp, qs, mp, ms, fp, fs = [], [], [], [], [], [], []
    s_sample = None
    for l in range(depth):
        alog_row = _pad_lanes(a_log[l][None])
        dtb_row = _pad_lanes(dt_bias[l][None])
        og_row = onorm_g[l][None]
        cb_row = conv_ffn_b[l][None]

        h = _rmsnorm(xp, norm1_g[l][None], BF16, rows_norm)
        h3 = h.reshape(b, seq, d)
        qk, nb_qk = _qkv_prompt(h3, w_t, l, conv_qkv_w[l], pad, dk, 0, 2 * key_w, key_w // MXU_COLS, True, BF16)
        v, nb_v = _qkv_prompt(h3, w_t, l, conv_qkv_w[l], pad, dk, 2 * key_w, val_w, 0, False, F32)
        nb_q = jnp.concatenate([nb_qk, nb_v], axis=-1)
        gcb, bcb = _gates_prompt(h3, w_t, o_ab, l, alog_row, dtb_row, pad, n_heads, dk)
        gout = _mm(h, w_rest, l, c_go, val_w, tm, 1024, "gout")
        y_b, nb_m = _mixb_prompt(h3, w_rest, (c_bg, c_cg, c_hb), l, conv_mix_w[l])
        o, s_new = _delta_prompt(qk, v, gcb, bcb, gout.reshape(b, seq, val_w), og_row, pad, n_heads, dk)
        merged = _merge(o.reshape(m, val_w), y_b.reshape(m, w_b), h, wua, wub, w_rest, c_za, c_zb, l, tm)
        xp = _mm_res(merged, wo, l, xp, tm, tn_o, "out_proj")
        h2 = _rmsnorm(xp, norm2_g[l][None], BF16, rows_norm)
        act, nb_f = _ffn_prompt(h2.reshape(b, seq, d), w_ffn_up, l, conv_ffn_w[l], cb_row)
        xp = _mm_res(act.reshape(m, d_ff), wfd, l, xp, tm, tn_f, "ffn_down")
        sp.append(s_new); qp.append(nb_q); mp.append(nb_m); fp.append(nb_f)

        hs = _rmsnorm(xs, norm1_g[l][None], BF16, bs)
        qkv_s, z_new = _qkv_sample(hs, w_t, l, conv_qkv_w[l], st_qkv, dk, key_w)
        gcb_s, bcb_s = _gates_sample(hs, w_t, o_ab, l, alog_row, dtb_row, n_heads, dk)
        gout_s = _mm(hs, w_rest, l, c_go, val_w, bs, 1024, "gout")
        yb_s, pre_new = _mixb_sample(hs, w_rest, (c_bg, c_cg, c_hb), l, conv_mix_w[l], st_mix)
        q3 = qkv_s[:, :key_w].reshape(bs, n_heads, dk)
        k3 = qkv_s[:, key_w:2 * key_w].reshape(bs, n_heads, dk)
        v3 = qkv_s[:, 2 * key_w:].reshape(bs, n_heads, dv)
        o_s, s_sample = _delta_sample(q3, k3, v3, gcb_s, bcb_s, gout_s.reshape(bs, n_heads, dv), og_row,
                                      state_delta, l, s_sample)
        merged_s = _merge(o_s.reshape(bs, val_w), yb_s, hs, wua, wub, w_rest, c_za, c_zb, l, bs)
        xs = _mm_res(merged_s, wo, l, xs, bs, tn_o, "out_proj")
        h2s = _rmsnorm(xs, norm2_g[l][None], BF16, bs)
        act_s, gate_new = _ffn_sample(h2s, w_ffn_up, l, conv_ffn_w[l], cb_row, st_ffn)
        xs = _mm_res(act_s, wfd, l, xs, bs, tn_f, "ffn_down")
        qs.append(_roll_state(state_conv_qkv[l], z_new))
        ms.append(_roll_state(state_conv_mix[l], pre_new))
        fs.append(_roll_state(state_conv_ffn[l], gate_new))

    y_prompt = _rmsnorm(xp, final_g[None], F32, rows_norm // 2).reshape(b, seq, d)[:, n_meta:]
    y_sample = _rmsnorm(xs, final_g[None], F32, bs).reshape(bs, 1, d)
    return (y_prompt, y_sample, jnp.stack(sp), s_sample, jnp.stack(qp), jnp.stack(qs),
            jnp.stack(mp), jnp.stack(ms), jnp.stack(fp), jnp.stack(fs))
```

```python
import functools

import jax
import jax.numpy as jnp
from jax import lax
from jax.experimental import pallas as pl
from jax.experimental.pallas import tpu as pltpu

F32 = jnp.float32
BF16 = jnp.bfloat16
EPS = 1e-6
HIGHEST = lax.Precision.HIGHEST
CHUNK = 128
LANES = 128
SUBLANES = 8
MXU_COLS = 256


def _sigmoid(x):
    return 1.0 / (1.0 + jnp.exp(-x))


def _silu(x):
    return x * _sigmoid(x)


def _softplus(x):
    return jnp.maximum(x, 0.0) + jnp.log1p(jnp.exp(-jnp.abs(x)))


def _dot(a, b, precision=None):
    return jnp.dot(a, b, preferred_element_type=F32, precision=precision)


def _dot_nt(a, wt):
    return lax.dot_general(a, wt.astype(BF16), (((1,), (1,)), ((), ())), preferred_element_type=F32)


def _call(kernel, *, grid, in_specs, out_specs, out_shape, scratch_shapes=(), name=None,
          input_output_aliases=None):
    return pl.pallas_call(
        kernel, grid=grid, in_specs=in_specs, out_specs=out_specs, out_shape=out_shape,
        scratch_shapes=list(scratch_shapes), name=name,
        input_output_aliases=input_output_aliases or {},
        compiler_params=pltpu.CompilerParams(dimension_semantics=("arbitrary",) * len(grid)))


def _rmsnorm_kernel(x_ref, g_ref, o_ref):
    x = x_ref[...]
    ms = jnp.mean(x * x, axis=-1, keepdims=True)
    o_ref[...] = ((x * lax.rsqrt(ms + EPS)) * g_ref[...]).astype(o_ref.dtype)


def _rmsnorm(x2d, g_row, out_dtype, rows):
    m, d = x2d.shape
    return _call(
        _rmsnorm_kernel, grid=(m // rows,),
        in_specs=[pl.BlockSpec((rows, d), lambda i: (i, 0)), pl.BlockSpec((1, d), lambda i: (0, 0))],
        out_specs=pl.BlockSpec((rows, d), lambda i: (i, 0)),
        out_shape=jax.ShapeDtypeStruct((m, d), out_dtype), name="rmsnorm")(x2d, g_row)


def _mm_kernel(a_ref, w_ref, o_ref):
    o_ref[...] = _dot_nt(a_ref[...], w_ref[...]).astype(o_ref.dtype)


def _mm(a, wt_stack, layer, row0, n, tm, tn, name):
    m, k = a.shape
    r0 = row0 // tn
    return _call(
        _mm_kernel, grid=(m // tm, n // tn),
        in_specs=[pl.BlockSpec((tm, k), lambda i, j: (i, 0)),
                  pl.BlockSpec((None, tn, k), lambda i, j: (layer, j + r0, 0))],
        out_specs=pl.BlockSpec((tm, tn), lambda i, j: (i, j)),
        out_shape=jax.ShapeDtypeStruct((m, n), F32), name=name)(a, wt_stack)


def _mm_res_kernel(a_ref, w_ref, r_ref, o_ref):
    o_ref[...] = r_ref[...] + _dot(a_ref[...], w_ref[...].astype(BF16))


def _mm_res(a, w_stack, layer, res, tm, tn, name):
    m, k = a.shape
    n = w_stack.shape[-1]
    return _call(
        _mm_res_kernel, grid=(m // tm, n // tn),
        in_specs=[pl.BlockSpec((tm, k), lambda i, j: (i, 0)),
                  pl.BlockSpec((None, k, tn), lambda i, j: (layer, 0, j)),
                  pl.BlockSpec((tm, tn), lambda i, j: (i, j))],
        out_specs=pl.BlockSpec((tm, tn), lambda i, j: (i, j)),
        out_shape=jax.ShapeDtypeStruct((m, n), F32), name=name)(a, w_stack, res)


def _l2norm_heads(x, dk, scale):
    parts = []
    for i in range(x.shape[1] // dk):
        p = x[:, i * dk:(i + 1) * dk]
        p = p * lax.rsqrt(jnp.sum(p * p, axis=-1, keepdims=True) + EPS)
        parts.append(p * scale if scale != 1.0 else p)
    return jnp.concatenate(parts, axis=-1)


def _qkv_finish(c, o_write, n, n_q, n_k, dk):
    if n_q == 0 and n_k == 0:
        o_write(c)
        return

    @pl.when(n < n_q)
    def _():
        o_write(_l2norm_heads(c, dk, dk ** -0.5))

    @pl.when(jnp.logical_and(n >= n_q, n < n_q + n_k))
    def _():
        o_write(_l2norm_heads(c, dk, 1.0))

    @pl.when(n >= n_q + n_k)
    def _():
        o_write(c)


def _conv_rows(zp_ref, cw_ref, seq, taps, cols=slice(None)):
    acc = zp_ref[SUBLANES:SUBLANES + seq, :] * cw_ref[taps - 1:taps, cols]
    for i in range(taps - 1):
        s = taps - 1 - i
        acc = acc + zp_ref[SUBLANES - s:SUBLANES - s + seq, :] * cw_ref[i:i + 1, cols]
    return acc


TILES_PER_STEP = 2


def _qkv_prompt_kernel(h_ref, w_ref, cw_ref, qkv_ref, nb_ref, *zs_refs, seq, pad, taps, q_steps, l2, dk):
    j = pl.program_id(1)
    tn = MXU_COLS
    h = h_ref[...]
    qkv_ref[0:pad, :] = jnp.zeros((pad, qkv_ref.shape[1]), qkv_ref.dtype)
    scale = jnp.where(j < q_steps, dk ** -0.5, 1.0)
    for t, zs_ref in enumerate(zs_refs):
        cols = slice(t * tn, (t + 1) * tn)
        zs_ref[0:SUBLANES, :] = jnp.zeros((SUBLANES, tn), F32)
        zs_ref[SUBLANES:SUBLANES + seq, :] = _dot_nt(h, w_ref[cols, :])
        nb_ref[:, cols] = zs_ref[SUBLANES + seq - (taps - 1):SUBLANES + seq, :]
        c = _silu(_conv_rows(zs_ref, cw_ref, seq, taps, cols))
        if l2:
            c = _l2norm_heads(c, dk, 1.0) * scale
        qkv_ref[pad:pad + seq, cols] = c.astype(qkv_ref.dtype)


def _qkv_prompt(h3, w_t, layer, cw, pad, dk, col0, width, n_q, l2, out_dtype):
    b, seq, d = h3.shape
    taps = cw.shape[0]
    tn = MXU_COLS
    tw = TILES_PER_STEP * tn
    assert col0 % tw == 0 and width % tw == 0 and n_q % TILES_PER_STEP == 0
    c0 = col0 // tw
    kern = functools.partial(_qkv_prompt_kernel, seq=seq, pad=pad, taps=taps, q_steps=n_q // TILES_PER_STEP,
                             l2=l2, dk=dk)
    return _call(
        kern, grid=(b, width // tw),
        in_specs=[pl.BlockSpec((None, seq, d), lambda i, j: (i, 0, 0), pipeline_mode=pl.Buffered(1)),
                  pl.BlockSpec((None, tw, d), lambda i, j: (layer, j + c0, 0)),
                  pl.BlockSpec((taps, tw), lambda i, j: (0, j + c0))],
        out_specs=[pl.BlockSpec((None, seq + pad, tw), lambda i, j: (i, 0, j)),
                   pl.BlockSpec((None, taps - 1, tw), lambda i, j: (i, 0, j))],
        out_shape=[jax.ShapeDtypeStruct((b, seq + pad, width), out_dtype),
                   jax.ShapeDtypeStruct((b, taps - 1, width), F32)],
        scratch_shapes=[pltpu.VMEM((seq + SUBLANES, tn), F32)] * TILES_PER_STEP,
        name="qkv_prompt")(h3, w_t, cw)


def _qkv_sample_kernel(h_ref, w_ref, cw_ref, st_ref, qkv_ref, z_ref, *, taps, n_q, n_k, dk):
    n = pl.program_id(0)
    z = _dot_nt(h_ref[...], w_ref[...])
    z_ref[...] = z
    acc = z * cw_ref[taps - 1:taps, :]
    for i in range(taps - 1):
        acc = acc + st_ref[i] * cw_ref[i:i + 1, :]
    c = _silu(acc)

    def write(val):
        qkv_ref[...] = val

    _qkv_finish(c, write, n, n_q, n_k, dk)


def _qkv_sample(h2, w_stack, layer, cw, st_stack, dk, key_w):
    bs, d = h2.shape
    taps, width = cw.shape
    tn = MXU_COLS
    kern = functools.partial(_qkv_sample_kernel, taps=taps, n_q=key_w // tn, n_k=key_w // tn, dk=dk)
    return _call(
        kern, grid=(width // tn,),
        in_specs=[pl.BlockSpec((bs, d), lambda j: (0, 0)),
                  pl.BlockSpec((None, tn, d), lambda j: (layer, j, 0)),
                  pl.BlockSpec((taps, tn), lambda j: (0, j)),
                  pl.BlockSpec((None, taps - 1, bs, tn), lambda j: (layer, 0, 0, j))],
        out_specs=[pl.BlockSpec((bs, tn), lambda j: (0, j)), pl.BlockSpec((bs, tn), lambda j: (0, j))],
        out_shape=[jax.ShapeDtypeStruct((bs, width), F32), jax.ShapeDtypeStruct((bs, width), F32)],
        name="qkv_sample")(h2, w_stack, cw, st_stack)


def _gates_math(ab, alog_ref, dtb_ref):
    g = -jnp.exp(alog_ref[...]) * _softplus(ab + dtb_ref[...])
    return g, _sigmoid(ab)


def _expand_heads(x, first_head, lane_off, heads, dk):
    src = lax.broadcasted_iota(jnp.int32, (LANES, heads * dk), 0)
    dst = lax.broadcasted_iota(jnp.int32, (LANES, heads * dk), 1) // dk + (first_head + lane_off)
    onehot = jnp.where(src == dst, 1.0, 0.0).astype(BF16)
    out = None
    for _ in range(3):
        piece = x.astype(BF16)
        x = x - piece.astype(F32)
        term = _dot(piece, onehot)
        out = term if out is None else out + term
    return out


def _gates_prompt_kernel(h_ref, wab_ref, alog_ref, dtb_ref, g_ref, b_ref, gp_ref, bp_ref,
                         *, seq, pad, n_heads, dk, heads_per_step):
    j = pl.program_id(1)

    @pl.when(j == 0)
    def _():
        ab = _dot_nt(h_ref[...], wab_ref[...])
        g, beta = _gates_math(ab, alog_ref, dtb_ref)
        zeros = jnp.zeros((pad, LANES), F32)
        gp_ref[0:pad, :] = zeros
        bp_ref[0:pad, :] = zeros
        gp_ref[pad:pad + seq, :] = g
        bp_ref[pad:pad + seq, :] = beta
        row = lax.broadcasted_iota(jnp.int32, (CHUNK, CHUNK), 0)
        col = lax.broadcasted_iota(jnp.int32, (CHUNK, CHUNK), 1)
        tril = (row >= col).astype(F32)
        for c in range((seq + pad) // CHUNK):
            rows = slice(c * CHUNK, (c + 1) * CHUNK)
            gp_ref[rows, :] = _dot(tril, gp_ref[rows, :], precision=HIGHEST)

    g_ref[...] = _expand_heads(gp_ref[...], j * heads_per_step, 0, heads_per_step, dk)
    b_ref[...] = _expand_heads(bp_ref[...], j * heads_per_step, n_heads, heads_per_step, dk)


def _gates_prompt(h3, w_t, ab_row0, layer, alog_row, dtb_row, pad, n_heads, dk):
    assert ab_row0 % LANES == 0
    b, seq, d = h3.shape
    t = seq + pad
    hps = 4
    kern = functools.partial(_gates_prompt_kernel, seq=seq, pad=pad, n_heads=n_heads, dk=dk, heads_per_step=hps)
    out = jax.ShapeDtypeStruct((b, t, n_heads * dk), F32)
    return _call(
        kern, grid=(b, n_heads // hps),
        in_specs=[pl.BlockSpec((None, seq, d), lambda i, j: (i, 0, 0), pipeline_mode=pl.Buffered(1)),
                  pl.BlockSpec((None, LANES, d), lambda i, j: (layer, ab_row0 // LANES, 0)),
                  pl.BlockSpec((1, LANES), lambda i, j: (0, 0)),
                  pl.BlockSpec((1, LANES), lambda i, j: (0, 0))],
        out_specs=[pl.BlockSpec((None, t, hps * dk), lambda i, j: (i, 0, j)),
                   pl.BlockSpec((None, t, hps * dk), lambda i, j: (i, 0, j))],
        out_shape=[out, out],
        scratch_shapes=[pltpu.VMEM((t, LANES), F32), pltpu.VMEM((t, LANES), F32)],
        name="gates_prompt")(h3, w_t, alog_row, dtb_row)


def _gates_sample_kernel(h_ref, wab_ref, alog_ref, dtb_ref, g_ref, b_ref, *, n_heads, dk):
    ab = _dot_nt(h_ref[...], wab_ref[...])
    g, beta = _gates_math(ab, alog_ref, dtb_ref)
    g_ref[...] = _expand_heads(g, 0, 0, n_heads, dk)
    b_ref[...] = _expand_heads(beta, 0, n_heads, n_heads, dk)


def _gates_sample(h2, w_t, ab_row0, layer, alog_row, dtb_row, n_heads, dk):
    bs, d = h2.shape
    out = jax.ShapeDtypeStruct((bs, n_heads * dk), F32)
    return _call(
        functools.partial(_gates_sample_kernel, n_heads=n_heads, dk=dk), grid=(1,),
        in_specs=[pl.BlockSpec((bs, d), lambda i: (0, 0)),
                  pl.BlockSpec((None, LANES, d), lambda i: (layer, ab_row0 // LANES, 0)),
                  pl.BlockSpec((1, LANES), lambda i: (0, 0)),
                  pl.BlockSpec((1, LANES), lambda i: (0, 0))],
        out_specs=[pl.BlockSpec((bs, n_heads * dk), lambda i: (0, 0)),
                   pl.BlockSpec((bs, n_heads * dk), lambda i: (0, 0))],
        out_shape=[out, out], name="gates_sample")(h2, w_t, alog_row, dtb_row)


def _mixb_prompt_kernel(h_ref, wb_ref, wc_ref, wh_ref, cw_ref, y_ref, nb_ref, ps_ref, *, seq, taps):
    h = h_ref[...]
    pre = _dot_nt(h, wc_ref[...]) * _dot_nt(h, wh_ref[...])
    ps_ref[0:SUBLANES, :] = jnp.zeros((SUBLANES, pre.shape[1]), F32)
    ps_ref[SUBLANES:SUBLANES + seq, :] = pre
    nb_ref[...] = ps_ref[SUBLANES + seq - (taps - 1):SUBLANES + seq, :]
    acc = pre * cw_ref[taps - 1:taps, :]
    for i in range(taps - 1):
        s = taps - 1 - i
        acc = acc + ps_ref[SUBLANES - s:SUBLANES - s + seq, :] * cw_ref[i:i + 1, :]
    y_ref[...] = (_dot_nt(h, wb_ref[...]) * acc).astype(y_ref.dtype)


def _mixb_prompt(h3, w_stack, cols, layer, cw):
    b, seq, d = h3.shape
    taps, width = cw.shape
    tn = MXU_COLS

    def wspec(col0):
        return pl.BlockSpec((None, tn, d), lambda i, j: (layer, j + col0 // tn, 0))

    return _call(
        functools.partial(_mixb_prompt_kernel, seq=seq, taps=taps), grid=(b, width // tn),
        in_specs=[pl.BlockSpec((None, seq, d), lambda i, j: (i, 0, 0), pipeline_mode=pl.Buffered(1)),
                  wspec(cols[0]), wspec(cols[1]), wspec(cols[2]), pl.BlockSpec((taps, tn), lambda i, j: (0, j))],
        out_specs=[pl.BlockSpec((None, seq, tn), lambda i, j: (i, 0, j)),
                   pl.BlockSpec((None, taps - 1, tn), lambda i, j: (i, 0, j))],
        out_shape=[jax.ShapeDtypeStruct((b, seq, width), BF16),
                   jax.ShapeDtypeStruct((b, taps - 1, width), F32)],
        scratch_shapes=[pltpu.VMEM((seq + SUBLANES, tn), F32)],
        name="mixb_prompt")(h3, w_stack, w_stack, w_stack, cw)


def _mixb_sample_kernel(h_ref, wb_ref, wc_ref, wh_ref, cw_ref, st_ref, y_ref, pre_ref, *, taps):
    h = h_ref[...]
    pre = _dot_nt(h, wc_ref[...]) * _dot_nt(h, wh_ref[...])
    pre_ref[...] = pre
    acc = pre * cw_ref[taps - 1:taps, :]
    for i in range(taps - 1):
        acc = acc + st_ref[i] * cw_ref[i:i + 1, :]
    y_ref[...] = (_dot_nt(h, wb_ref[...]) * acc).astype(y_ref.dtype)


def _mixb_sample(h2, w_stack, cols, layer, cw, st_stack):
    bs, d = h2.shape
    taps, width = cw.shape
    tn = MXU_COLS

    def wspec(col0):
        return pl.BlockSpec((None, tn, d), lambda j: (layer, j + col0 // tn, 0))

    return _call(
        functools.partial(_mixb_sample_kernel, taps=taps), grid=(width // tn,),
        in_specs=[pl.BlockSpec((bs, d), lambda j: (0, 0)), wspec(cols[0]), wspec(cols[1]), wspec(cols[2]),
                  pl.BlockSpec((taps, tn), lambda j: (0, j)),
                  pl.BlockSpec((None, taps - 1, bs, tn), lambda j: (layer, 0, 0, j))],
        out_specs=[pl.BlockSpec((bs, tn), lambda j: (0, j)), pl.BlockSpec((bs, tn), lambda j: (0, j))],
        out_shape=[jax.ShapeDtypeStruct((bs, width), BF16), jax.ShapeDtypeStruct((bs, width), F32)],
        name="mixb_sample")(h2, w_stack, w_stack, w_stack, cw, st_stack)


def _bmm(a, b):
    return jnp.einsum("cij,cjk->cik", a.astype(BF16), b.astype(BF16), preferred_element_type=F32)


def _bmm_nt(a, b):
    return jnp.einsum("cid,cjd->cij", a.astype(BF16), b.astype(BF16), preferred_element_type=F32)


def _unit_lower_inverse(lm, row, col, eye):
    base = 16
    in_blk = (row // base) == (col // base)
    ld = jnp.where(in_blk, lm, 0.0)
    x = eye.astype(F32) - ld
    m = ld
    for _ in range(3):
        m = _bmm(m, m)
        x = x + _bmm(x, m)
    size = base
    while size < CHUNK:
        size *= 2
        in_big = (row // size) == (col // size)
        off = jnp.where(jnp.logical_and(in_big, jnp.logical_not(in_blk)), lm, 0.0)
        x = x - _bmm(x, _bmm(off, x))
        in_blk = in_big
    return x


HEADS_PER_STEP = 2


def _delta_prompt_kernel(q_ref, k_ref, v_ref, g_ref, b_ref, go_ref, og_ref, o_ref, s_ref,
                         w_s, u_s, kt_s, qk_s, oo_s, *, seq, pad, dk):
    t = seq + pad
    nc = t // CHUNK
    row = lax.broadcasted_iota(jnp.int32, (CHUNK, CHUNK), 0)
    col = lax.broadcasted_iota(jnp.int32, (CHUNK, CHUNK), 1)
    eye = row == col
    incl = row >= col
    for hh in range(HEADS_PER_STEP):
        lanes = slice(hh * dk, (hh + 1) * dk)
        q3 = q_ref[:, lanes].reshape(nc, CHUNK, dk)
        k3 = k_ref[:, lanes].reshape(nc, CHUNK, dk)
        v3 = v_ref[:, lanes].reshape(nc, CHUNK, dk)
        g3 = g_ref[:, lanes].reshape(nc, CHUNK, dk)
        b3 = b_ref[:, lanes].reshape(nc, CHUNK, dk)
        kf = k3.astype(F32)
        gi = g3[:, :, :CHUNK]
        gj = jnp.sum(jnp.where(eye, gi, 0.0), axis=1, keepdims=True)
        decay = jnp.where(incl, jnp.exp(jnp.where(incl, gi - gj, 0.0)), 0.0)
        lm = jnp.where(row > col, b3[:, :, :CHUNK] * _bmm_nt(k3, k3) * decay, 0.0)
        tinv = _unit_lower_inverse(lm, row, col, eye)
        u_s[hh] = _bmm(tinv, b3 * v3).reshape(t, dk)
        w_s[hh] = _bmm(tinv, b3 * jnp.exp(g3) * kf).reshape(t, dk).astype(BF16)
        qk_s[hh] = (_bmm_nt(q3, k3) * decay).reshape(t, CHUNK).astype(BF16)
        kt_s[hh] = (kf * jnp.exp(g3[:, CHUNK - 1:CHUNK, :] - g3)).reshape(t, dk).astype(BF16)

    def body(c, states):
        r0 = pl.multiple_of(c * CHUNK, CHUNK)
        rows = pl.ds(r0, CHUNK)
        new_states = []
        for hh in range(HEADS_PER_STEP):
            lanes = slice(hh * dk, (hh + 1) * dk)
            s = states[hh]
            sb = s.astype(BF16)
            u = u_s[hh, rows, :] - _dot(w_s[hh, rows, :], sb)
            ub = u.astype(BF16)
            oo_s[hh, rows, :] = (jnp.exp(g_ref[rows, lanes]) * _dot(q_ref[rows, lanes], sb)
                                 + _dot(qk_s[hh, rows, :], ub))
            g_tail = g_ref[pl.ds(pl.multiple_of(r0 + CHUNK - SUBLANES, SUBLANES), SUBLANES), lanes]
            s_scale = jnp.exp(g_tail[SUBLANES - 1:SUBLANES, :])
            new_states.append(s_scale * s + lax.dot_general(kt_s[hh, rows, :], ub, (((0,), (0,)), ((), ())),
                                                            preferred_element_type=F32))
        return tuple(new_states)

    init = tuple(jnp.zeros((dk, dk), F32) for _ in range(HEADS_PER_STEP))
    states = lax.fori_loop(0, nc, body, init)
    for hh in range(HEADS_PER_STEP):
        lanes = slice(hh * dk, (hh + 1) * dk)
        s_ref[hh] = states[hh]
        o = oo_s[hh, pad:pad + seq, :]
        o = (o * lax.rsqrt(jnp.mean(o * o, axis=-1, keepdims=True) + EPS)) * og_ref[...]
        o_ref[:, lanes] = (o * _silu(go_ref[:, lanes])).astype(o_ref.dtype)


def _delta_prompt(qk, v, gcb, bcb, gout3, og_row, pad, n_heads, dk):
    b, t, _ = v.shape
    seq = t - pad
    hps = HEADS_PER_STEP
    wd = hps * dk

    def col_spec(rows, off):
        return pl.BlockSpec((None, rows, wd), lambda i, j: (i, 0, j + off))

    kern = functools.partial(_delta_prompt_kernel, seq=seq, pad=pad, dk=dk)
    return _call(
        kern, grid=(b, n_heads // hps),
        in_specs=[col_spec(t, 0), col_spec(t, n_heads // hps), col_spec(t, 0),
                  col_spec(t, 0), col_spec(t, 0), col_spec(seq, 0),
                  pl.BlockSpec((1, dk), lambda i, j: (0, 0))],
        out_specs=[col_spec(seq, 0), pl.BlockSpec((None, hps, dk, dk), lambda i, j: (i, j, 0, 0))],
        out_shape=[jax.ShapeDtypeStruct((b, seq, n_heads * dk), BF16),
                   jax.ShapeDtypeStruct((b, n_heads, dk, dk), F32)],
        scratch_shapes=[pltpu.VMEM((hps, t, dk), BF16), pltpu.VMEM((hps, t, dk), F32),
                        pltpu.VMEM((hps, t, dk), BF16), pltpu.VMEM((hps, t, CHUNK), BF16),
                        pltpu.VMEM((hps, t, dk), F32)],
        name="delta_prompt")(qk, qk, v, gcb, bcb, gout3, og_row)


def _delta_sample_kernel(q_ref, k_ref, v_ref, g_ref, b_ref, go_ref, og_ref, s_ref, *rest, n_heads, dk):
    o_ref, sn_ref = rest[-2:]
    for bi in range(SAMPLES_PER_STEP):
        q = q_ref[bi]
        k = k_ref[bi]
        v = v_ref[bi]
        kq = jnp.concatenate([k, q, jnp.zeros((LANES - 2 * n_heads, dk), F32)], axis=0)
        kq_t = kq.T
        qk = jnp.sum(q * k, axis=-1, keepdims=True)
        outs = []
        for h in range(n_heads):
            s = s_ref[bi, h]
            a = jnp.exp(g_ref[bi, :, h * dk:(h + 1) * dk])
            beta = b_ref[bi, :, h * dk:(h + 1) * dk]
            kcol = kq_t[:, h:h + 1]
            qcol = kq_t[:, n_heads + h:n_heads + h + 1]
            ks = jnp.sum(kcol * s, axis=0, keepdims=True)
            qs = jnp.sum(qcol * s, axis=0, keepdims=True)
            u = beta * (v[h:h + 1, :] - a * ks)
            outs.append(a * qs + qk[h:h + 1, :] * u)
            sn_ref[bi, h] = a * s + kcol * u
        o = jnp.concatenate(outs, axis=0)
        o = (o * lax.rsqrt(jnp.mean(o * o, axis=-1, keepdims=True) + EPS)) * og_ref[...]
        o_ref[bi] = (o * _silu(go_ref[bi])).astype(o_ref.dtype)


SAMPLES_PER_STEP = 2


def _delta_sample(q3, k3, v3, gcb, bcb, gout3, og_row, state, layer, prev_out):
    bs, n_heads, dk = q3.shape
    depth = state.shape[0]
    nb = SAMPLES_PER_STEP
    hspec = pl.BlockSpec((nb, n_heads, dk), lambda i: (i, 0, 0))
    rspec = pl.BlockSpec((nb, 1, n_heads * dk), lambda i: (i, 0, 0))
    sspec = pl.BlockSpec((None, nb, n_heads, dk, dk), lambda i: (layer, i, 0, 0, 0))
    in_specs = [hspec, hspec, hspec, rspec, rspec, hspec, pl.BlockSpec((1, dk), lambda i: (0, 0)), sspec]
    args = [q3, k3, v3, gcb.reshape(bs, 1, -1), bcb.reshape(bs, 1, -1), gout3, og_row, state]
    aliases = {}
    if prev_out is not None:
        aliases = {len(args): 1}
        in_specs.append(pl.BlockSpec(memory_space=pl.ANY))
        args.append(prev_out)
    return _call(
        functools.partial(_delta_sample_kernel, n_heads=n_heads, dk=dk), grid=(bs // nb,),
        in_specs=in_specs, out_specs=[hspec, sspec],
        out_shape=[jax.ShapeDtypeStruct((bs, n_heads, dk), BF16),
                   jax.ShapeDtypeStruct((depth, bs, n_heads, dk, dk), F32)],
        input_output_aliases=aliases, name="delta_sample")(*args)


def _merge_kernel(o_ref, y_ref, h_ref, wa_ref, wb_ref, wza_ref, wzb_ref, m_ref):
    h = h_ref[...]
    za = _sigmoid(_dot_nt(h, wza_ref[...]))
    zb = _sigmoid(_dot_nt(h, wzb_ref[...]))
    pa = _dot(o_ref[...], wa_ref[...].astype(BF16))
    pb = _dot(y_ref[...], wb_ref[...].astype(BF16))
    m_ref[...] = (za * pa + zb * pb).astype(m_ref.dtype)


def _merge(o2, y2, h2, wa_stack, wb_stack, wz_stack, col_za, col_zb, layer, tm):
    m, d = h2.shape
    ka = o2.shape[1]
    kb = y2.shape[1]
    n = wa_stack.shape[-1]
    tn = MXU_COLS

    def wspec(kk):
        return pl.BlockSpec((None, kk, tn), lambda i, j: (layer, 0, j))

    def wtspec(row0):
        return pl.BlockSpec((None, tn, d), lambda i, j: (layer, j + row0 // tn, 0))

    return _call(
        _merge_kernel, grid=(m // tm, n // tn),
        in_specs=[pl.BlockSpec((tm, ka), lambda i, j: (i, 0)), pl.BlockSpec((tm, kb), lambda i, j: (i, 0)),
                  pl.BlockSpec((tm, d), lambda i, j: (i, 0)), wspec(ka), wspec(kb),
                  wtspec(col_za), wtspec(col_zb)],
        out_specs=pl.BlockSpec((tm, tn), lambda i, j: (i, j)),
        out_shape=jax.ShapeDtypeStruct((m, n), BF16), name="merge")(
            o2, y2, h2, wa_stack, wb_stack, wz_stack, wz_stack)


def _ffn_prompt_kernel(h_ref, wg_ref, wv_ref, cw_ref, cb_ref, a_ref, nb_ref, gs_ref, *, seq, taps):
    h = h_ref[...]
    gs_ref[0:SUBLANES, :] = jnp.zeros((SUBLANES, gs_ref.shape[1]), F32)
    gs_ref[SUBLANES:SUBLANES + seq, :] = _dot(h, wg_ref[...].astype(BF16))
    nb_ref[...] = gs_ref[SUBLANES + seq - (taps - 1):SUBLANES + seq, :]
    acc = _conv_rows(gs_ref, cw_ref, seq, taps)
    a_ref[...] = (_silu(acc + cb_ref[...]) * _dot(h, wv_ref[...].astype(BF16))).astype(a_ref.dtype)


def _ffn_prompt(h3, wup_stack, layer, cw, cb_row):
    b, seq, d = h3.shape
    dff = cw.shape[1]
    taps = cw.shape[0]
    tn = MXU_COLS
    nt = dff // tn
    return _call(
        functools.partial(_ffn_prompt_kernel, seq=seq, taps=taps), grid=(b, nt),
        in_specs=[pl.BlockSpec((None, seq, d), lambda i, j: (i, 0, 0), pipeline_mode=pl.Buffered(1)),
                  pl.BlockSpec((None, d, tn), lambda i, j: (layer, 0, j)),
                  pl.BlockSpec((None, d, tn), lambda i, j: (layer, 0, j + nt)),
                  pl.BlockSpec((taps, tn), lambda i, j: (0, j)),
                  pl.BlockSpec((1, tn), lambda i, j: (0, j))],
        out_specs=[pl.BlockSpec((None, seq, tn), lambda i, j: (i, 0, j)),
                   pl.BlockSpec((None, taps - 1, tn), lambda i, j: (i, 0, j))],
        out_shape=[jax.ShapeDtypeStruct((b, seq, dff), BF16),
                   jax.ShapeDtypeStruct((b, taps - 1, dff), F32)],
        scratch_shapes=[pltpu.VMEM((seq + SUBLANES, tn), F32)],
        name="ffn_prompt")(h3, wup_stack, wup_stack, cw, cb_row)


def _ffn_sample_kernel(h_ref, wg_ref, wv_ref, cw_ref, cb_ref, st_ref, a_ref, g_ref, *, taps):
    h = h_ref[...]
    gate = _dot(h, wg_ref[...].astype(BF16))
    g_ref[...] = gate
    acc = gate * cw_ref[taps - 1:taps, :]
    for i in range(taps - 1):
        acc = acc + st_ref[i] * cw_ref[i:i + 1, :]
    a_ref[...] = (_silu(acc + cb_ref[...]) * _dot(h, wv_ref[...].astype(BF16))).astype(a_ref.dtype)


def _ffn_sample(h2, wup_stack, layer, cw, cb_row, st_stack):
    bs, d = h2.shape
    dff = cw.shape[1]
    taps = cw.shape[0]
    tn = MXU_COLS
    nt = dff // tn
    return _call(
        functools.partial(_ffn_sample_kernel, taps=taps), grid=(nt,),
        in_specs=[pl.BlockSpec((bs, d), lambda j: (0, 0)),
                  pl.BlockSpec((None, d, tn), lambda j: (layer, 0, j)),
                  pl.BlockSpec((None, d, tn), lambda j: (layer, 0, j + nt)),
                  pl.BlockSpec((taps, tn), lambda j: (0, j)),
                  pl.BlockSpec((1, tn), lambda j: (0, j)),
                  pl.BlockSpec((None, taps - 1, bs, tn), lambda j: (layer, 0, 0, j))],
        out_specs=[pl.BlockSpec((bs, tn), lambda j: (0, j)), pl.BlockSpec((bs, tn), lambda j: (0, j))],
        out_shape=[jax.ShapeDtypeStruct((bs, dff), BF16), jax.ShapeDtypeStruct((bs, dff), F32)],
        name="ffn_sample")(h2, wup_stack, wup_stack, cw, cb_row, st_stack)


def _cast_kernel(x_ref, o_ref):
    o_ref[...] = x_ref[0].astype(o_ref.dtype)


def _cast_rows_bf16(w_t, row0, rows):
    depth, _, k = w_t.shape
    tr = 512
    assert row0 % 16 == 0 and rows % tr == 0
    return _call(
        _cast_kernel, grid=(depth, rows // tr),
        in_specs=[pl.BlockSpec((pl.Element(1), pl.Element(tr), pl.Element(k)),
                               lambda l, i: (l, pl.multiple_of(row0 + i * tr, 16), 0))],
        out_specs=pl.BlockSpec((None, tr, k), lambda l, i: (l, i, 0)),
        out_shape=jax.ShapeDtypeStruct((depth, rows, k), BF16), name="cast_rows")(w_t)


def _pad_lanes(x2d):
    return jnp.pad(x2d, ((0, 0), (0, LANES - x2d.shape[1])))


def _roll_state(state_l, new_row):
    return jnp.concatenate([state_l[:, 1:], new_row[:, None]], axis=1)


def kernel(x_prompt, x_sample, state_delta, state_conv_qkv, state_conv_mix, state_conv_ffn, meta_tokens,
           norm1_g, w_in, conv_qkv_w, a_log, dt_bias, onorm_g, conv_mix_w, w_up_a, w_up_b, w_o, norm2_g,
           w_ffn_up, conv_ffn_w, conv_ffn_b, w_ffn_down, final_g):
    b, seq0, d = x_prompt.shape
    bs = x_sample.shape[0]
    depth, _, n_heads, dk, dv = state_delta.shape
    assert dk == dv == LANES and x_sample.shape[1] == 1
    n_meta = meta_tokens.shape[0]
    seq = seq0 + n_meta
    pad = (-seq) % CHUNK
    key_w = n_heads * dk
    val_w = n_heads * dv
    qkv_w = state_conv_qkv.shape[-1]
    w_b = state_conv_mix.shape[-1]
    d_ff = state_conv_ffn.shape[-1]
    assert 2 * n_heads <= LANES

    o_ab = qkv_w
    o_go = o_ab + 2 * n_heads
    o_bg = o_go + val_w
    o_cg = o_bg + w_b
    o_hb = o_cg + w_b
    o_za = o_hb + w_b
    o_zb = o_za + d
    assert o_zb + d == w_in.shape[-1]

    w_t = jnp.swapaxes(w_in, 1, 2)
    w_rest = _cast_rows_bf16(w_t, o_go, w_in.shape[-1] - o_go)
    c_go, c_bg, c_cg, c_hb, c_za, c_zb = (o - o_go for o in (o_go, o_bg, o_cg, o_hb, o_za, o_zb))
    wua, wub = w_up_a, w_up_b
    wo, wfd = w_o.astype(BF16), w_ffn_down.astype(BF16)
    st_qkv = jnp.swapaxes(state_conv_qkv, 1, 2)
    st_mix = jnp.swapaxes(state_conv_mix, 1, 2)
    st_ffn = jnp.swapaxes(state_conv_ffn, 1, 2)

    xp = jnp.concatenate([jnp.broadcast_to(meta_tokens[None], (b, n_meta, d)), x_prompt], axis=1)
    xp = xp.reshape(b * seq, d)
    xs = x_sample.reshape(bs, d)
    m = b * seq
    tm = seq // 3
    rows_norm = tm
    assert seq % 3 == 0 and tm % 16 == 0
    tn_o = 512
    tn_f = MXU_COLS

    sp, qp, qs, mp, ms, fp, fs = [], [], [], [], [], [], []
    s_sample = None
    for l in range(depth):
        alog_row = _pad_lanes(a_log[l][None])
        dtb_row = _pad_lanes(dt_bias[l][None])
        og_row = onorm_g[l][None]
        cb_row = conv_ffn_b[l][None]

        h = _rmsnorm(xp, norm1_g[l][None], BF16, rows_norm)
        h3 = h.reshape(b, seq, d)
        qk, nb_qk = _qkv_prompt(h3, w_t, l, conv_qkv_w[l], pad, dk, 0, 2 * key_w, key_w // MXU_COLS, True, BF16)
        v, nb_v = _qkv_prompt(h3, w_t, l, conv_qkv_w[l], pad, dk, 2 * key_w, val_w, 0, False, F32)
        nb_q = jnp.concatenate([nb_qk, nb_v], axis=-1)
        gcb, bcb = _gates_prompt(h3, w_t, o_ab, l, alog_row, dtb_row, pad, n_heads, dk)
        gout = _mm(h, w_rest, l, c_go, val_w, tm, 1024, "gout")
        y_b, nb_m = _mixb_prompt(h3, w_rest, (c_bg, c_cg, c_hb), l, conv_mix_w[l])
        o, s_new = _delta_prompt(qk, v, gcb, bcb, gout.reshape(b, seq, val_w), og_row, pad, n_heads, dk)
        merged = _merge(o.reshape(m, val_w), y_b.reshape(m, w_b), h, wua, wub, w_rest, c_za, c_zb, l, tm)
        xp = _mm_res(merged, wo, l, xp, tm, tn_o, "out_proj")
        h2 = _rmsnorm(xp, norm2_g[l][None], BF16, rows_norm)
        act, nb_f = _ffn_prompt(h2.reshape(b, seq, d), w_ffn_up, l, conv_ffn_w[l], cb_row)
        xp = _mm_res(act.reshape(m, d_ff), wfd, l, xp, tm, tn_f, "ffn_down")
        sp.append(s_new); qp.append(nb_q); mp.append(nb_m); fp.append(nb_f)

        hs = _rmsnorm(xs, norm1_g[l][None], BF16, bs)
        qkv_s, z_new = _qkv_sample(hs, w_t, l, conv_qkv_w[l], st_qkv, dk, key_w)
        gcb_s, bcb_s = _gates_sample(hs, w_t, o_ab, l, alog_row, dtb_row, n_heads, dk)
        gout_s = _mm(hs, w_rest, l, c_go, val_w, bs, 1024, "gout")
        yb_s, pre_new = _mixb_sample(hs, w_rest, (c_bg, c_cg, c_hb), l, conv_mix_w[l], st_mix)
        q3 = qkv_s[:, :key_w].reshape(bs, n_heads, dk)
        k3 = qkv_s[:, key_w:2 * key_w].reshape(bs, n_heads, dk)
        v3 = qkv_s[:, 2 * key_w:].reshape(bs, n_heads, dv)
        o_s, s_sample = _delta_sample(q3, k3, v3, gcb_s, bcb_s, gout_s.reshape(bs, n_heads, dv), og_row,
                                      state_delta, l, s_sample)
        merged_s = _merge(o_s.reshape(bs, val_w), yb_s, hs, wua, wub, w_rest, c_za, c_zb, l, bs)
        xs = _mm_res(merged_s, wo, l, xs, bs, tn_o, "out_proj")
        h2s = _rmsnorm(xs, norm2_g[l][None], BF16, bs)
        act_s, gate_new = _ffn_sample(h2s, w_ffn_up, l, conv_ffn_w[l], cb_row, st_ffn)
        xs = _mm_res(act_s, wfd, l, xs, bs, tn_f, "ffn_down")
        qs.append(_roll_state(state_conv_qkv[l], z_new))
        ms.append(_roll_state(state_conv_mix[l], pre_new))
        fs.append(_roll_state(state_conv_ffn[l], gate_new))

    y_prompt = _rmsnorm(xp, final_g[None], F32, rows_norm // 2).reshape(b, seq, d)[:, n_meta:]
    y_sample = _rmsnorm(xs, final_g[None], F32, bs).reshape(bs, 1, d)
    return (y_prompt, y_sample, jnp.stack(sp), s_sample, jnp.stack(qp), jnp.stack(qs),
            jnp.stack(mp), jnp.stack(ms), jnp.stack(fp), jnp.stack(fs))
```

```python
import functools

import jax
import jax.numpy as jnp
from jax import lax
from jax.experimental import pallas as pl
from jax.experimental.pallas import tpu as pltpu

F32 = jnp.float32
BF16 = jnp.bfloat16
EPS = 1e-6
HIGHEST = lax.Precision.HIGHEST
CHUNK = 128
LANES = 128
SUBLANES = 8
MXU_COLS = 256


def _sigmoid(x):
    return 1.0 / (1.0 + jnp.exp(-x))


def _silu(x):
    return x * _sigmoid(x)


def _softplus(x):
    return jnp.maximum(x, 0.0) + jnp.log1p(jnp.exp(-jnp.abs(x)))


def _dot(a, b, precision=None):
    return jnp.dot(a, b, preferred_element_type=F32, precision=precision)


def _dot_nt(a, wt):
    return lax.dot_general(a, wt.astype(BF16), (((1,), (1,)), ((), ())), preferred_element_type=F32)


def _call(kernel, *, grid, in_specs, out_specs, out_shape, scratch_shapes=(), name=None,
          input_output_aliases=None):
    return pl.pallas_call(
        kernel, grid=grid, in_specs=in_specs, out_specs=out_specs, out_shape=out_shape,
        scratch_shapes=list(scratch_shapes), name=name,
        input_output_aliases=input_output_aliases or {},
        compiler_params=pltpu.CompilerParams(dimension_semantics=("arbitrary",) * len(grid)))


def _rmsnorm_kernel(x_ref, g_ref, o_ref):
    x = x_ref[...]
    ms = jnp.mean(x * x, axis=-1, keepdims=True)
    o_ref[...] = ((x * lax.rsqrt(ms + EPS)) * g_ref[...]).astype(o_ref.dtype)


def _rmsnorm(x2d, g_row, out_dtype, rows):
    m, d = x2d.shape
    return _call(
        _rmsnorm_kernel, grid=(m // rows,),
        in_specs=[pl.BlockSpec((rows, d), lambda i: (i, 0)), pl.BlockSpec((1, d), lambda i: (0, 0))],
        out_specs=pl.BlockSpec((rows, d), lambda i: (i, 0)),
        out_shape=jax.ShapeDtypeStruct((m, d), out_dtype), name="rmsnorm")(x2d, g_row)


def _mm_kernel(a_ref, w_ref, o_ref):
    o_ref[...] = _dot_nt(a_ref[...], w_ref[...]).astype(o_ref.dtype)


def _mm(a, wt_stack, layer, row0, n, tm, tn, name):
    m, k = a.shape
    r0 = row0 // tn
    return _call(
        _mm_kernel, grid=(m // tm, n // tn),
        in_specs=[pl.BlockSpec((tm, k), lambda i, j: (i, 0)),
                  pl.BlockSpec((None, tn, k), lambda i, j: (layer, j + r0, 0))],
        out_specs=pl.BlockSpec((tm, tn), lambda i, j: (i, j)),
        out_shape=jax.ShapeDtypeStruct((m, n), F32), name=name)(a, wt_stack)


def _mm_res_kernel(a_ref, w_ref, r_ref, o_ref):
    o_ref[...] = r_ref[...] + _dot(a_ref[...], w_ref[...].astype(BF16))


def _mm_res(a, w_stack, layer, res, tm, tn, name):
    m, k = a.shape
    n = w_stack.shape[-1]
    return _call(
        _mm_res_kernel, grid=(m // tm, n // tn),
        in_specs=[pl.BlockSpec((tm, k), lambda i, j: (i, 0)),
                  pl.BlockSpec((None, k, tn), lambda i, j: (layer, 0, j)),
                  pl.BlockSpec((tm, tn), lambda i, j: (i, j))],
        out_specs=pl.BlockSpec((tm, tn), lambda i, j: (i, j)),
        out_shape=jax.ShapeDtypeStruct((m, n), F32), name=name)(a, w_stack, res)


def _mm_res_norm_kernel(a_ref, w_ref, r_ref, g_ref, o_ref, h_ref, xs_ref, ss_ref, *, n_tiles, tn):
    j = pl.program_id(1)
    x = r_ref[...] + _dot(a_ref[...], w_ref[...].astype(BF16))
    o_ref[...] = x
    xs_ref[j] = x
    part = jnp.sum(x * x, axis=-1, keepdims=True)

    @pl.when(j == 0)
    def _():
        ss_ref[...] = part

    @pl.when(j > 0)
    def _():
        ss_ref[...] += part

    @pl.when(j == n_tiles - 1)
    def _():
        scale = lax.rsqrt(ss_ref[...] * (1.0 / (n_tiles * tn)) + EPS)
        for t in range(n_tiles):
            cols = slice(t * tn, (t + 1) * tn)
            h_ref[:, cols] = ((xs_ref[t] * scale) * g_ref[:, cols]).astype(h_ref.dtype)


def _mm_res_norm(a, w_stack, layer, res, g_row, tm, tn, name):
    m, k = a.shape
    n = w_stack.shape[-1]
    nt = n // tn
    return _call(
        functools.partial(_mm_res_norm_kernel, n_tiles=nt, tn=tn), grid=(m // tm, nt),
        in_specs=[pl.BlockSpec((tm, k), lambda i, j: (i, 0)),
                  pl.BlockSpec((None, k, tn), lambda i, j: (layer, 0, j)),
                  pl.BlockSpec((tm, tn), lambda i, j: (i, j)),
                  pl.BlockSpec((1, n), lambda i, j: (0, 0))],
        out_specs=[pl.BlockSpec((tm, tn), lambda i, j: (i, j)), pl.BlockSpec((tm, n), lambda i, j: (i, 0))],
        out_shape=[jax.ShapeDtypeStruct((m, n), F32), jax.ShapeDtypeStruct((m, n), BF16)],
        scratch_shapes=[pltpu.VMEM((nt, tm, tn), F32), pltpu.VMEM((tm, 1), F32)],
        name=name)(a, w_stack, res, g_row)


def _l2norm_heads(x, dk, scale):
    parts = []
    for i in range(x.shape[1] // dk):
        p = x[:, i * dk:(i + 1) * dk]
        p = p * lax.rsqrt(jnp.sum(p * p, axis=-1, keepdims=True) + EPS)
        parts.append(p * scale if scale != 1.0 else p)
    return jnp.concatenate(parts, axis=-1)


def _qkv_finish(c, o_write, n, n_q, n_k, dk):
    if n_q == 0 and n_k == 0:
        o_write(c)
        return

    @pl.when(n < n_q)
    def _():
        o_write(_l2norm_heads(c, dk, dk ** -0.5))

    @pl.when(jnp.logical_and(n >= n_q, n < n_q + n_k))
    def _():
        o_write(_l2norm_heads(c, dk, 1.0))

    @pl.when(n >= n_q + n_k)
    def _():
        o_write(c)


def _conv_rows(zp_ref, cw_ref, seq, taps, cols=slice(None)):
    acc = zp_ref[SUBLANES:SUBLANES + seq, :] * cw_ref[taps - 1:taps, cols]
    for i in range(taps - 1):
        s = taps - 1 - i
        acc = acc + zp_ref[SUBLANES - s:SUBLANES - s + seq, :] * cw_ref[i:i + 1, cols]
    return acc


TILES_PER_STEP = 2


def _qkv_prompt_kernel(h_ref, w_ref, cw_ref, qkv_ref, nb_ref, *zs_refs, seq, pad, taps, q_steps, l2, dk):
    j = pl.program_id(1)
    tn = MXU_COLS
    h = h_ref[...]
    qkv_ref[0:pad, :] = jnp.zeros((pad, qkv_ref.shape[1]), qkv_ref.dtype)
    scale = jnp.where(j < q_steps, dk ** -0.5, 1.0)
    for t, zs_ref in enumerate(zs_refs):
        cols = slice(t * tn, (t + 1) * tn)
        zs_ref[0:SUBLANES, :] = jnp.zeros((SUBLANES, tn), F32)
        zs_ref[SUBLANES:SUBLANES + seq, :] = _dot_nt(h, w_ref[cols, :])
        nb_ref[:, cols] = zs_ref[SUBLANES + seq - (taps - 1):SUBLANES + seq, :]
        c = _silu(_conv_rows(zs_ref, cw_ref, seq, taps, cols))
        if l2:
            c = _l2norm_heads(c, dk, 1.0) * scale
        qkv_ref[pad:pad + seq, cols] = c.astype(qkv_ref.dtype)


def _qkv_prompt(h3, w_t, layer, cw, pad, dk, col0, width, n_q, l2, out_dtype):
    b, seq, d = h3.shape
    taps = cw.shape[0]
    tn = MXU_COLS
    tw = TILES_PER_STEP * tn
    assert col0 % tw == 0 and width % tw == 0 and n_q % TILES_PER_STEP == 0
    c0 = col0 // tw
    kern = functools.partial(_qkv_prompt_kernel, seq=seq, pad=pad, taps=taps, q_steps=n_q // TILES_PER_STEP,
                             l2=l2, dk=dk)
    return _call(
        kern, grid=(b, width // tw),
        in_specs=[pl.BlockSpec((None, seq, d), lambda i, j: (i, 0, 0), pipeline_mode=pl.Buffered(1)),
                  pl.BlockSpec((None, tw, d), lambda i, j: (layer, j + c0, 0)),
                  pl.BlockSpec((taps, tw), lambda i, j: (0, j + c0))],
        out_specs=[pl.BlockSpec((None, seq + pad, tw), lambda i, j: (i, 0, j)),
                   pl.BlockSpec((None, taps - 1, tw), lambda i, j: (i, 0, j))],
        out_shape=[jax.ShapeDtypeStruct((b, seq + pad, width), out_dtype),
                   jax.ShapeDtypeStruct((b, taps - 1, width), F32)],
        scratch_shapes=[pltpu.VMEM((seq + SUBLANES, tn), F32)] * TILES_PER_STEP,
        name="qkv_prompt")(h3, w_t, cw)


def _qkv_sample_kernel(h_ref, w_ref, cw_ref, st_ref, qkv_ref, z_ref, *, taps, n_q, n_k, dk):
    n = pl.program_id(0)
    z = _dot_nt(h_ref[...], w_ref[...])
    z_ref[...] = z
    acc = z * cw_ref[taps - 1:taps, :]
    for i in range(taps - 1):
        acc = acc + st_ref[i] * cw_ref[i:i + 1, :]
    c = _silu(acc)

    def write(val):
        qkv_ref[...] = val

    _qkv_finish(c, write, n, n_q, n_k, dk)


def _qkv_sample(h2, w_stack, layer, cw, st_stack, dk, key_w):
    bs, d = h2.shape
    taps, width = cw.shape
    tn = MXU_COLS
    kern = functools.partial(_qkv_sample_kernel, taps=taps, n_q=key_w // tn, n_k=key_w // tn, dk=dk)
    return _call(
        kern, grid=(width // tn,),
        in_specs=[pl.BlockSpec((bs, d), lambda j: (0, 0)),
                  pl.BlockSpec((None, tn, d), lambda j: (layer, j, 0)),
                  pl.BlockSpec((taps, tn), lambda j: (0, j)),
                  pl.BlockSpec((None, taps - 1, bs, tn), lambda j: (layer, 0, 0, j))],
        out_specs=[pl.BlockSpec((bs, tn), lambda j: (0, j)), pl.BlockSpec((bs, tn), lambda j: (0, j))],
        out_shape=[jax.ShapeDtypeStruct((bs, width), F32), jax.ShapeDtypeStruct((bs, width), F32)],
        name="qkv_sample")(h2, w_stack, cw, st_stack)


def _gates_math(ab, alog_ref, dtb_ref):
    g = -jnp.exp(alog_ref[...]) * _softplus(ab + dtb_ref[...])
    return g, _sigmoid(ab)


def _expand_heads(x, first_head, lane_off, heads, dk):
    src = lax.broadcasted_iota(jnp.int32, (LANES, heads * dk), 0)
    dst = lax.broadcasted_iota(jnp.int32, (LANES, heads * dk), 1) // dk + (first_head + lane_off)
    onehot = jnp.where(src == dst, 1.0, 0.0).astype(BF16)
    out = None
    for _ in range(3):
        piece = x.astype(BF16)
        x = x - piece.astype(F32)
        term = _dot(piece, onehot)
        out = term if out is None else out + term
    return out


def _gates_prompt_kernel(h_ref, wab_ref, alog_ref, dtb_ref, g_ref, b_ref, gp_ref, bp_ref,
                         *, seq, pad, n_heads, dk, heads_per_step):
    j = pl.program_id(1)

    @pl.when(j == 0)
    def _():
        ab = _dot_nt(h_ref[...], wab_ref[...])
        g, beta = _gates_math(ab, alog_ref, dtb_ref)
        zeros = jnp.zeros((pad, LANES), F32)
        gp_ref[0:pad, :] = zeros
        bp_ref[0:pad, :] = zeros
        gp_ref[pad:pad + seq, :] = g
        bp_ref[pad:pad + seq, :] = beta
        row = lax.broadcasted_iota(jnp.int32, (CHUNK, CHUNK), 0)
        col = lax.broadcasted_iota(jnp.int32, (CHUNK, CHUNK), 1)
        tril = (row >= col).astype(F32)
        for c in range((seq + pad) // CHUNK):
            rows = slice(c * CHUNK, (c + 1) * CHUNK)
            gp_ref[rows, :] = _dot(tril, gp_ref[rows, :], precision=HIGHEST)

    g_ref[...] = _expand_heads(gp_ref[...], j * heads_per_step, 0, heads_per_step, dk)
    b_ref[...] = _expand_heads(bp_ref[...], j * heads_per_step, n_heads, heads_per_step, dk)


def _gates_prompt(h3, w_t, ab_row0, layer, alog_row, dtb_row, pad, n_heads, dk):
    assert ab_row0 % LANES == 0
    b, seq, d = h3.shape
    t = seq + pad
    hps = 4
    kern = functools.partial(_gates_prompt_kernel, seq=seq, pad=pad, n_heads=n_heads, dk=dk, heads_per_step=hps)
    out = jax.ShapeDtypeStruct((b, t, n_heads * dk), F32)
    return _call(
        kern, grid=(b, n_heads // hps),
        in_specs=[pl.BlockSpec((None, seq, d), lambda i, j: (i, 0, 0), pipeline_mode=pl.Buffered(1)),
                  pl.BlockSpec((None, LANES, d), lambda i, j: (layer, ab_row0 // LANES, 0)),
                  pl.BlockSpec((1, LANES), lambda i, j: (0, 0)),
                  pl.BlockSpec((1, LANES), lambda i, j: (0, 0))],
        out_specs=[pl.BlockSpec((None, t, hps * dk), lambda i, j: (i, 0, j)),
                   pl.BlockSpec((None, t, hps * dk), lambda i, j: (i, 0, j))],
        out_shape=[out, out],
        scratch_shapes=[pltpu.VMEM((t, LANES), F32), pltpu.VMEM((t, LANES), F32)],
        name="gates_prompt")(h3, w_t, alog_row, dtb_row)


def _gates_sample_kernel(h_ref, wab_ref, alog_ref, dtb_ref, g_ref, b_ref, *, n_heads, dk):
    ab = _dot_nt(h_ref[...], wab_ref[...])
    g, beta = _gates_math(ab, alog_ref, dtb_ref)
    g_ref[...] = _expand_heads(g, 0, 0, n_heads, dk)
    b_ref[...] = _expand_heads(beta, 0, n_heads, n_heads, dk)


def _gates_sample(h2, w_t, ab_row0, layer, alog_row, dtb_row, n_heads, dk):
    bs, d = h2.shape
    out = jax.ShapeDtypeStruct((bs, n_heads * dk), F32)
    return _call(
        functools.partial(_gates_sample_kernel, n_heads=n_heads, dk=dk), grid=(1,),
        in_specs=[pl.BlockSpec((bs, d), lambda i: (0, 0)),
                  pl.BlockSpec((None, LANES, d), lambda i: (layer, ab_row0 // LANES, 0)),
                  pl.BlockSpec((1, LANES), lambda i: (0, 0)),
                  pl.BlockSpec((1, LANES), lambda i: (0, 0))],
        out_specs=[pl.BlockSpec((bs, n_heads * dk), lambda i: (0, 0)),
                   pl.BlockSpec((bs, n_heads * dk), lambda i: (0, 0))],
        out_shape=[out, out], name="gates_sample")(h2, w_t, alog_row, dtb_row)


def _mixb_prompt_kernel(h_ref, wb_ref, wc_ref, wh_ref, cw_ref, y_ref, nb_ref, ps_ref, *, seq, taps):
    h = h_ref[...]
    pre = _dot_nt(h, wc_ref[...]) * _dot_nt(h, wh_ref[...])
    ps_ref[0:SUBLANES, :] = jnp.zeros((SUBLANES, pre.shape[1]), F32)
    ps_ref[SUBLANES:SUBLANES + seq, :] = pre
    nb_ref[...] = ps_ref[SUBLANES + seq - (taps - 1):SUBLANES + seq, :]
    acc = pre * cw_ref[taps - 1:taps, :]
    for i in range(taps - 1):
        s = taps - 1 - i
        acc = acc + ps_ref[SUBLANES - s:SUBLANES - s + seq, :] * cw_ref[i:i + 1, :]
    y_ref[...] = (_dot_nt(h, wb_ref[...]) * acc).astype(y_ref.dtype)


def _mixb_prompt(h3, w_stack, cols, layer, cw):
    b, seq, d = h3.shape
    taps, width = cw.shape
    tn = MXU_COLS

    def wspec(col0):
        return pl.BlockSpec((None, tn, d), lambda i, j: (layer, j + col0 // tn, 0))

    return _call(
        functools.partial(_mixb_prompt_kernel, seq=seq, taps=taps), grid=(b, width // tn),
        in_specs=[pl.BlockSpec((None, seq, d), lambda i, j: (i, 0, 0), pipeline_mode=pl.Buffered(1)),
                  wspec(cols[0]), wspec(cols[1]), wspec(cols[2]), pl.BlockSpec((taps, tn), lambda i, j: (0, j))],
        out_specs=[pl.BlockSpec((None, seq, tn), lambda i, j: (i, 0, j)),
                   pl.BlockSpec((None, taps - 1, tn), lambda i, j: (i, 0, j))],
        out_shape=[jax.ShapeDtypeStruct((b, seq, width), BF16),
                   jax.ShapeDtypeStruct((b, taps - 1, width), F32)],
        scratch_shapes=[pltpu.VMEM((seq + SUBLANES, tn), F32)],
        name="mixb_prompt")(h3, w_stack, w_stack, w_stack, cw)


def _mixb_sample_kernel(h_ref, wb_ref, wc_ref, wh_ref, cw_ref, st_ref, y_ref, pre_ref, *, taps):
    h = h_ref[...]
    pre = _dot_nt(h, wc_ref[...]) * _dot_nt(h, wh_ref[...])
    pre_ref[...] = pre
    acc = pre * cw_ref[taps - 1:taps, :]
    for i in range(taps - 1):
        acc = acc + st_ref[i] * cw_ref[i:i + 1, :]
    y_ref[...] = (_dot_nt(h, wb_ref[...]) * acc).astype(y_ref.dtype)


def _mixb_sample(h2, w_stack, cols, layer, cw, st_stack):
    bs, d = h2.shape
    taps, width = cw.shape
    tn = MXU_COLS

    def wspec(col0):
        return pl.BlockSpec((None, tn, d), lambda j: (layer, j + col0 // tn, 0))

    return _call(
        functools.partial(_mixb_sample_kernel, taps=taps), grid=(width // tn,),
        in_specs=[pl.BlockSpec((bs, d), lambda j: (0, 0)), wspec(cols[0]), wspec(cols[1]), wspec(cols[2]),
                  pl.BlockSpec((taps, tn), lambda j: (0, j)),
                  pl.BlockSpec((None, taps - 1, bs, tn), lambda j: (layer, 0, 0, j))],
        out_specs=[pl.BlockSpec((bs, tn), lambda j: (0, j)), pl.BlockSpec((bs, tn), lambda j: (0, j))],
        out_shape=[jax.ShapeDtypeStruct((bs, width), BF16), jax.ShapeDtypeStruct((bs, width), F32)],
        name="mixb_sample")(h2, w_stack, w_stack, w_stack, cw, st_stack)


def _bmm(a, b):
    return jnp.einsum("cij,cjk->cik", a.astype(BF16), b.astype(BF16), preferred_element_type=F32)


def _bmm_nt(a, b):
    return jnp.einsum("cid,cjd->cij", a.astype(BF16), b.astype(BF16), preferred_element_type=F32)


def _unit_lower_inverse(lm, row, col, eye):
    base = 16
    in_blk = (row // base) == (col // base)
    ld = jnp.where(in_blk, lm, 0.0)
    x = eye.astype(F32) - ld
    m = ld
    for _ in range(3):
        m = _bmm(m, m)
        x = x + _bmm(x, m)
    size = base
    while size < CHUNK:
        size *= 2
        in_big = (row // size) == (col // size)
        off = jnp.where(jnp.logical_and(in_big, jnp.logical_not(in_blk)), lm, 0.0)
        x = x - _bmm(x, _bmm(off, x))
        in_blk = in_big
    return x


HEADS_PER_STEP = 2


def _delta_prompt_kernel(q_ref, k_ref, v_ref, g_ref, b_ref, go_ref, og_ref, o_ref, s_ref,
                         w_s, u_s, kt_s, qk_s, oo_s, *, seq, pad, dk):
    t = seq + pad
    nc = t // CHUNK
    row = lax.broadcasted_iota(jnp.int32, (CHUNK, CHUNK), 0)
    col = lax.broadcasted_iota(jnp.int32, (CHUNK, CHUNK), 1)
    eye = row == col
    incl = row >= col
    for hh in range(HEADS_PER_STEP):
        lanes = slice(hh * dk, (hh + 1) * dk)
        q3 = q_ref[:, lanes].reshape(nc, CHUNK, dk)
        k3 = k_ref[:, lanes].reshape(nc, CHUNK, dk)
        v3 = v_ref[:, lanes].reshape(nc, CHUNK, dk)
        g3 = g_ref[:, lanes].reshape(nc, CHUNK, dk)
        b3 = b_ref[:, lanes].reshape(nc, CHUNK, dk)
        kf = k3.astype(F32)
        gi = g3[:, :, :CHUNK]
        gj = jnp.sum(jnp.where(eye, gi, 0.0), axis=1, keepdims=True)
        decay = jnp.where(incl, jnp.exp(jnp.where(incl, gi - gj, 0.0)), 0.0)
        lm = jnp.where(row > col, b3[:, :, :CHUNK] * _bmm_nt(k3, k3) * decay, 0.0)
        tinv = _unit_lower_inverse(lm, row, col, eye)
        u_s[hh] = _bmm(tinv, b3 * v3).reshape(t, dk)
        w_s[hh] = _bmm(tinv, b3 * jnp.exp(g3) * kf).reshape(t, dk).astype(BF16)
        qk_s[hh] = (_bmm_nt(q3, k3) * decay).reshape(t, CHUNK).astype(BF16)
        kt_s[hh] = (kf * jnp.exp(g3[:, CHUNK - 1:CHUNK, :] - g3)).reshape(t, dk).astype(BF16)

    def body(c, states):
        r0 = pl.multiple_of(c * CHUNK, CHUNK)
        rows = pl.ds(r0, CHUNK)
        new_states = []
        for hh in range(HEADS_PER_STEP):
            lanes = slice(hh * dk, (hh + 1) * dk)
            s = states[hh]
            sb = s.astype(BF16)
            u = u_s[hh, rows, :] - _dot(w_s[hh, rows, :], sb)
            ub = u.astype(BF16)
            oo_s[hh, rows, :] = (jnp.exp(g_ref[rows, lanes]) * _dot(q_ref[rows, lanes], sb)
                                 + _dot(qk_s[hh, rows, :], ub))
            g_tail = g_ref[pl.ds(pl.multiple_of(r0 + CHUNK - SUBLANES, SUBLANES), SUBLANES), lanes]
            s_scale = jnp.exp(g_tail[SUBLANES - 1:SUBLANES, :])
            new_states.append(s_scale * s + lax.dot_general(kt_s[hh, rows, :], ub, (((0,), (0,)), ((), ())),
                                                            preferred_element_type=F32))
        return tuple(new_states)

    init = tuple(jnp.zeros((dk, dk), F32) for _ in range(HEADS_PER_STEP))
    states = lax.fori_loop(0, nc, body, init)
    for hh in range(HEADS_PER_STEP):
        lanes = slice(hh * dk, (hh + 1) * dk)
        s_ref[hh] = states[hh]
        o = oo_s[hh, pad:pad + seq, :]
        o = (o * lax.rsqrt(jnp.mean(o * o, axis=-1, keepdims=True) + EPS)) * og_ref[...]
        o_ref[:, lanes] = (o * _silu(go_ref[:, lanes])).astype(o_ref.dtype)


def _delta_prompt(qk, v, gcb, bcb, gout3, og_row, pad, n_heads, dk):
    b, t, _ = v.shape
    seq = t - pad
    hps = HEADS_PER_STEP
    wd = hps * dk

    def col_spec(rows, off):
        return pl.BlockSpec((None, rows, wd), lambda i, j: (i, 0, j + off))

    kern = functools.partial(_delta_prompt_kernel, seq=seq, pad=pad, dk=dk)
    return _call(
        kern, grid=(b, n_heads // hps),
        in_specs=[col_spec(t, 0), col_spec(t, n_heads // hps), col_spec(t, 0),
                  col_spec(t, 0), col_spec(t, 0), col_spec(seq, 0),
                  pl.BlockSpec((1, dk), lambda i, j: (0, 0))],
        out_specs=[col_spec(seq, 0), pl.BlockSpec((None, hps, dk, dk), lambda i, j: (i, j, 0, 0))],
        out_shape=[jax.ShapeDtypeStruct((b, seq, n_heads * dk), BF16),
                   jax.ShapeDtypeStruct((b, n_heads, dk, dk), F32)],
        scratch_shapes=[pltpu.VMEM((hps, t, dk), BF16), pltpu.VMEM((hps, t, dk), F32),
                        pltpu.VMEM((hps, t, dk), BF16), pltpu.VMEM((hps, t, CHUNK), BF16),
                        pltpu.VMEM((hps, t, dk), F32)],
        name="delta_prompt")(qk, qk, v, gcb, bcb, gout3, og_row)


def _delta_sample_kernel(q_ref, k_ref, v_ref, g_ref, b_ref, go_ref, og_ref, s_ref, *rest, n_heads, dk):
    o_ref, sn_ref = rest[-2:]
    for bi in range(SAMPLES_PER_STEP):
        q = q_ref[bi]
        k = k_ref[bi]
        v = v_ref[bi]
        kq = jnp.concatenate([k, q, jnp.zeros((LANES - 2 * n_heads, dk), F32)], axis=0)
        kq_t = kq.T
        qk = jnp.sum(q * k, axis=-1, keepdims=True)
        outs = []
        for h in range(n_heads):
            s = s_ref[bi, h]
            a = jnp.exp(g_ref[bi, :, h * dk:(h + 1) * dk])
            beta = b_ref[bi, :, h * dk:(h + 1) * dk]
            kcol = kq_t[:, h:h + 1]
            qcol = kq_t[:, n_heads + h:n_heads + h + 1]
            ks = jnp.sum(kcol * s, axis=0, keepdims=True)
            qs = jnp.sum(qcol * s, axis=0, keepdims=True)
            u = beta * (v[h:h + 1, :] - a * ks)
            outs.append(a * qs + qk[h:h + 1, :] * u)
            sn_ref[bi, h] = a * s + kcol * u
        o = jnp.concatenate(outs, axis=0)
        o = (o * lax.rsqrt(jnp.mean(o * o, axis=-1, keepdims=True) + EPS)) * og_ref[...]
        o_ref[bi] = (o * _silu(go_ref[bi])).astype(o_ref.dtype)


SAMPLES_PER_STEP = 4


def _delta_sample(q3, k3, v3, gcb, bcb, gout3, og_row, state, layer, prev_out):
    bs, n_heads, dk = q3.shape
    depth = state.shape[0]
    nb = SAMPLES_PER_STEP
    hspec = pl.BlockSpec((nb, n_heads, dk), lambda i: (i, 0, 0))
    rspec = pl.BlockSpec((nb, 1, n_heads * dk), lambda i: (i, 0, 0))
    sspec = pl.BlockSpec((None, nb, n_heads, dk, dk), lambda i: (layer, i, 0, 0, 0))
    in_specs = [hspec, hspec, hspec, rspec, rspec, hspec, pl.BlockSpec((1, dk), lambda i: (0, 0)), sspec]
    args = [q3, k3, v3, gcb.reshape(bs, 1, -1), bcb.reshape(bs, 1, -1), gout3, og_row, state]
    aliases = {}
    if prev_out is not None:
        aliases = {len(args): 1}
        in_specs.append(pl.BlockSpec(memory_space=pl.ANY))
        args.append(prev_out)
    return _call(
        functools.partial(_delta_sample_kernel, n_heads=n_heads, dk=dk), grid=(bs // nb,),
        in_specs=in_specs, out_specs=[hspec, sspec],
        out_shape=[jax.ShapeDtypeStruct((bs, n_heads, dk), BF16),
                   jax.ShapeDtypeStruct((depth, bs, n_heads, dk, dk), F32)],
        input_output_aliases=aliases, name="delta_sample")(*args)


def _merge_kernel(o_ref, y_ref, h_ref, wa_ref, wb_ref, wza_ref, wzb_ref, m_ref):
    h = h_ref[...]
    za = _sigmoid(_dot_nt(h, wza_ref[...]))
    zb = _sigmoid(_dot_nt(h, wzb_ref[...]))
    pa = _dot(o_ref[...], wa_ref[...].astype(BF16))
    pb = _dot(y_ref[...], wb_ref[...].astype(BF16))
    m_ref[...] = (za * pa + zb * pb).astype(m_ref.dtype)


def _merge(o2, y2, h2, wa_stack, wb_stack, wz_stack, col_za, col_zb, layer, tm):
    m, d = h2.shape
    ka = o2.shape[1]
    kb = y2.shape[1]
    n = wa_stack.shape[-1]
    tn = MXU_COLS

    def wspec(kk):
        return pl.BlockSpec((None, kk, tn), lambda i, j: (layer, 0, j))

    def wtspec(row0):
        return pl.BlockSpec((None, tn, d), lambda i, j: (layer, j + row0 // tn, 0))

    return _call(
        _merge_kernel, grid=(m // tm, n // tn),
        in_specs=[pl.BlockSpec((tm, ka), lambda i, j: (i, 0)), pl.BlockSpec((tm, kb), lambda i, j: (i, 0)),
                  pl.BlockSpec((tm, d), lambda i, j: (i, 0)), wspec(ka), wspec(kb),
                  wtspec(col_za), wtspec(col_zb)],
        out_specs=pl.BlockSpec((tm, tn), lambda i, j: (i, j)),
        out_shape=jax.ShapeDtypeStruct((m, n), BF16), name="merge")(
            o2, y2, h2, wa_stack, wb_stack, wz_stack, wz_stack)


def _ffn_prompt_kernel(h_ref, wg_ref, wv_ref, cw_ref, cb_ref, a_ref, nb_ref, gs_ref, *, seq, taps):
    h = h_ref[...]
    gs_ref[0:SUBLANES, :] = jnp.zeros((SUBLANES, gs_ref.shape[1]), F32)
    gs_ref[SUBLANES:SUBLANES + seq, :] = _dot(h, wg_ref[...].astype(BF16))
    nb_ref[...] = gs_ref[SUBLANES + seq - (taps - 1):SUBLANES + seq, :]
    acc = _conv_rows(gs_ref, cw_ref, seq, taps)
    a_ref[...] = (_silu(acc + cb_ref[...]) * _dot(h, wv_ref[...].astype(BF16))).astype(a_ref.dtype)


def _ffn_prompt(h3, wup_stack, layer, cw, cb_row):
    b, seq, d = h3.shape
    dff = cw.shape[1]
    taps = cw.shape[0]
    tn = MXU_COLS
    nt = dff // tn
    return _call(
        functools.partial(_ffn_prompt_kernel, seq=seq, taps=taps), grid=(b, nt),
        in_specs=[pl.BlockSpec((None, seq, d), lambda i, j: (i, 0, 0), pipeline_mode=pl.Buffered(1)),
                  pl.BlockSpec((None, d, tn), lambda i, j: (layer, 0, j)),
                  pl.BlockSpec((None, d, tn), lambda i, j: (layer, 0, j + nt)),
                  pl.BlockSpec((taps, tn), lambda i, j: (0, j)),
                  pl.BlockSpec((1, tn), lambda i, j: (0, j))],
        out_specs=[pl.BlockSpec((None, seq, tn), lambda i, j: (i, 0, j)),
                   pl.BlockSpec((None, taps - 1, tn), lambda i, j: (i, 0, j))],
        out_shape=[jax.ShapeDtypeStruct((b, seq, dff), BF16),
                   jax.ShapeDtypeStruct((b, taps - 1, dff), F32)],
        scratch_shapes=[pltpu.VMEM((seq + SUBLANES, tn), F32)],
        name="ffn_prompt")(h3, wup_stack, wup_stack, cw, cb_row)


def _ffn_sample_kernel(h_ref, wg_ref, wv_ref, cw_ref, cb_ref, st_ref, a_ref, g_ref, *, taps):
    h = h_ref[...]
    gate = _dot(h, wg_ref[...].astype(BF16))
    g_ref[...] = gate
    acc = gate * cw_ref[taps - 1:taps, :]
    for i in range(taps - 1):
        acc = acc + st_ref[i] * cw_ref[i:i + 1, :]
    a_ref[...] = (_silu(acc + cb_ref[...]) * _dot(h, wv_ref[...].astype(BF16))).astype(a_ref.dtype)


def _ffn_sample(h2, wup_stack, layer, cw, cb_row, st_stack):
    bs, d = h2.shape
    dff = cw.shape[1]
    taps = cw.shape[0]
    tn = MXU_COLS
    nt = dff // tn
    return _call(
        functools.partial(_ffn_sample_kernel, taps=taps), grid=(nt,),
        in_specs=[pl.BlockSpec((bs, d), lambda j: (0, 0)),
                  pl.BlockSpec((None, d, tn), lambda j: (layer, 0, j)),
                  pl.BlockSpec((None, d, tn), lambda j: (layer, 0, j + nt)),
                  pl.BlockSpec((taps, tn), lambda j: (0, j)),
                  pl.BlockSpec((1, tn), lambda j: (0, j)),
                  pl.BlockSpec((None, taps - 1, bs, tn), lambda j: (layer, 0, 0, j))],
        out_specs=[pl.BlockSpec((bs, tn), lambda j: (0, j)), pl.BlockSpec((bs, tn), lambda j: (0, j))],
        out_shape=[jax.ShapeDtypeStruct((bs, dff), BF16), jax.ShapeDtypeStruct((bs, dff), F32)],
        name="ffn_sample")(h2, wup_stack, wup_stack, cw, cb_row, st_stack)


def _cast_kernel(x_ref, o_ref):
    o_ref[...] = x_ref[0].astype(o_ref.dtype)


def _cast_rows_bf16(w_t, row0, rows):
    depth, _, k = w_t.shape
    tr = 512
    assert row0 % 16 == 0 and rows % tr == 0
    return _call(
        _cast_kernel, grid=(depth, rows // tr),
        in_specs=[pl.BlockSpec((pl.Element(1), pl.Element(tr), pl.Element(k)),
                               lambda l, i: (l, pl.multiple_of(row0 + i * tr, 16), 0))],
        out_specs=pl.BlockSpec((None, tr, k), lambda l, i: (l, i, 0)),
        out_shape=jax.ShapeDtypeStruct((depth, rows, k), BF16), name="cast_rows")(w_t)


def _pad_lanes(x2d):
    return jnp.pad(x2d, ((0, 0), (0, LANES - x2d.shape[1])))


def _roll_state(state_l, new_row):
    return jnp.concatenate([state_l[:, 1:], new_row[:, None]], axis=1)


def kernel(x_prompt, x_sample, state_delta, state_conv_qkv, state_conv_mix, state_conv_ffn, meta_tokens,
           norm1_g, w_in, conv_qkv_w, a_log, dt_bias, onorm_g, conv_mix_w, w_up_a, w_up_b, w_o, norm2_g,
           w_ffn_up, conv_ffn_w, conv_ffn_b, w_ffn_down, final_g):
    b, seq0, d = x_prompt.shape
    bs = x_sample.shape[0]
    depth, _, n_heads, dk, dv = state_delta.shape
    assert dk == dv == LANES and x_sample.shape[1] == 1
    n_meta = meta_tokens.shape[0]
    seq = seq0 + n_meta
    pad = (-seq) % CHUNK
    key_w = n_heads * dk
    val_w = n_heads * dv
    qkv_w = state_conv_qkv.shape[-1]
    w_b = state_conv_mix.shape[-1]
    d_ff = state_conv_ffn.shape[-1]
    assert 2 * n_heads <= LANES

    o_ab = qkv_w
    o_go = o_ab + 2 * n_heads
    o_bg = o_go + val_w
    o_cg = o_bg + w_b
    o_hb = o_cg + w_b
    o_za = o_hb + w_b
    o_zb = o_za + d
    assert o_zb + d == w_in.shape[-1]

    w_t = jnp.swapaxes(w_in, 1, 2)
    w_rest = _cast_rows_bf16(w_t, o_go, w_in.shape[-1] - o_go)
    c_go, c_bg, c_cg, c_hb, c_za, c_zb = (o - o_go for o in (o_go, o_bg, o_cg, o_hb, o_za, o_zb))
    wua, wub = w_up_a, w_up_b
    wo, wfd = w_o.astype(BF16), w_ffn_down.astype(BF16)
    st_qkv = jnp.swapaxes(state_conv_qkv, 1, 2)
    st_mix = jnp.swapaxes(state_conv_mix, 1, 2)
    st_ffn = jnp.swapaxes(state_conv_ffn, 1, 2)

    xp = jnp.concatenate([jnp.broadcast_to(meta_tokens[None], (b, n_meta, d)), x_prompt], axis=1)
    xp = xp.reshape(b * seq, d)
    xs = x_sample.reshape(bs, d)
    m = b * seq
    tm = seq // 3
    rows_norm = tm
    assert seq % 3 == 0 and tm % 16 == 0
    tn_o = 512
    tn_f = MXU_COLS

    sp, qp, qs, mp, ms, fp, fs = [], [], [], [], [], [], []
    s_sample = None
    for l in range(depth):
        alog_row = _pad_lanes(a_log[l][None])
        dtb_row = _pad_lanes(dt_bias[l][None])
        og_row = onorm_g[l][None]
        cb_row = conv_ffn_b[l][None]

        h = _rmsnorm(xp, norm1_g[l][None], BF16, rows_norm)
        h3 = h.reshape(b, seq, d)
        qk, nb_qk = _qkv_prompt(h3, w_t, l, conv_qkv_w[l], pad, dk, 0, 2 * key_w, key_w // MXU_COLS, True, BF16)
        v, nb_v = _qkv_prompt(h3, w_t, l, conv_qkv_w[l], pad, dk, 2 * key_w, val_w, 0, False, F32)
        nb_q = jnp.concatenate([nb_qk, nb_v], axis=-1)
        gcb, bcb = _gates_prompt(h3, w_t, o_ab, l, alog_row, dtb_row, pad, n_heads, dk)
        gout = _mm(h, w_rest, l, c_go, val_w, tm, 1024, "gout")
        y_b, nb_m = _mixb_prompt(h3, w_rest, (c_bg, c_cg, c_hb), l, conv_mix_w[l])
        o, s_new = _delta_prompt(qk, v, gcb, bcb, gout.reshape(b, seq, val_w), og_row, pad, n_heads, dk)
        merged = _merge(o.reshape(m, val_w), y_b.reshape(m, w_b), h, wua, wub, w_rest, c_za, c_zb, l, tm)
        xp, h2 = _mm_res_norm(merged, wo, l, xp, norm2_g[l][None], tm, tn_o, "out_proj")
        act, nb_f = _ffn_prompt(h2.reshape(b, seq, d), w_ffn_up, l, conv_ffn_w[l], cb_row)
        xp = _mm_res(act.reshape(m, d_ff), wfd, l, xp, tm, tn_f, "ffn_down")
        sp.append(s_new); qp.append(nb_q); mp.append(nb_m); fp.append(nb_f)

        hs = _rmsnorm(xs, norm1_g[l][None], BF16, bs)
        qkv_s, z_new = _qkv_sample(hs, w_t, l, conv_qkv_w[l], st_qkv, dk, key_w)
        gcb_s, bcb_s = _gates_sample(hs, w_t, o_ab, l, alog_row, dtb_row, n_heads, dk)
        gout_s = _mm(hs, w_rest, l, c_go, val_w, bs, 1024, "gout")
        yb_s, pre_new = _mixb_sample(hs, w_rest, (c_bg, c_cg, c_hb), l, conv_mix_w[l], st_mix)
        q3 = qkv_s[:, :key_w].reshape(bs, n_heads, dk)
        k3 = qkv_s[:, key_w:2 * key_w].reshape(bs, n_heads, dk)
        v3 = qkv_s[:, 2 * key_w:].reshape(bs, n_heads, dv)
        o_s, s_sample = _delta_sample(q3, k3, v3, gcb_s, bcb_s, gout_s.reshape(bs, n_heads, dv), og_row,
                                      state_delta, l, s_sample)
        merged_s = _merge(o_s.reshape(bs, val_w), yb_s, hs, wua, wub, w_rest, c_za, c_zb, l, bs)
        xs, h2s = _mm_res_norm(merged_s, wo, l, xs, norm2_g[l][None], bs, tn_o, "out_proj")
        act_s, gate_new = _ffn_sample(h2s, w_ffn_up, l, conv_ffn_w[l], cb_row, st_ffn)
        xs = _mm_res(act_s, wfd, l, xs, bs, tn_f, "ffn_down")
        qs.append(_roll_state(state_conv_qkv[l], z_new))
        ms.append(_roll_state(state_conv_mix[l], pre_new))
        fs.append(_roll_state(state_conv_ffn[l], gate_new))

    y_prompt = _rmsnorm(xp, final_g[None], F32, rows_norm // 2).reshape(b, seq, d)[:, n_meta:]
    y_sample = _rmsnorm(xs, final_g[None], F32, bs).reshape(bs, 1, d)
    return (y_prompt, y_sample, jnp.stack(sp), s_sample, jnp.stack(qp), jnp.stack(qs),
            jnp.stack(mp), jnp.stack(ms), jnp.stack(fp), jnp.stack(fs))
```

```python
import functools

import jax
import jax.numpy as jnp
from jax import lax
from jax.experimental import pallas as pl
from jax.experimental.pallas import tpu as pltpu

F32 = jnp.float32
BF16 = jnp.bfloat16
EPS = 1e-6
HIGHEST = lax.Precision.HIGHEST
CHUNK = 128
LANES = 128
SUBLANES = 8
MXU_COLS = 256


def _sigmoid(x):
    return 1.0 / (1.0 + jnp.exp(-x))


def _silu(x):
    return x * _sigmoid(x)


def _softplus(x):
    return jnp.maximum(x, 0.0) + jnp.log1p(jnp.exp(-jnp.abs(x)))


def _dot(a, b, precision=None):
    return jnp.dot(a, b, preferred_element_type=F32, precision=precision)


def _dot_nt(a, wt):
    return lax.dot_general(a, wt.astype(BF16), (((1,), (1,)), ((), ())), preferred_element_type=F32)


def _call(kernel, *, grid, in_specs, out_specs, out_shape, scratch_shapes=(), name=None,
          input_output_aliases=None):
    return pl.pallas_call(
        kernel, grid=grid, in_specs=in_specs, out_specs=out_specs, out_shape=out_shape,
        scratch_shapes=list(scratch_shapes), name=name,
        input_output_aliases=input_output_aliases or {},
        compiler_params=pltpu.CompilerParams(dimension_semantics=("arbitrary",) * len(grid)))


def _rmsnorm_kernel(x_ref, g_ref, o_ref):
    x = x_ref[...]
    ms = jnp.mean(x * x, axis=-1, keepdims=True)
    o_ref[...] = ((x * lax.rsqrt(ms + EPS)) * g_ref[...]).astype(o_ref.dtype)


def _rmsnorm(x2d, g_row, out_dtype, rows):
    m, d = x2d.shape
    return _call(
        _rmsnorm_kernel, grid=(m // rows,),
        in_specs=[pl.BlockSpec((rows, d), lambda i: (i, 0)), pl.BlockSpec((1, d), lambda i: (0, 0))],
        out_specs=pl.BlockSpec((rows, d), lambda i: (i, 0)),
        out_shape=jax.ShapeDtypeStruct((m, d), out_dtype), name="rmsnorm")(x2d, g_row)


def _mm_kernel(a_ref, w_ref, o_ref):
    o_ref[...] = _dot_nt(a_ref[...], w_ref[...]).astype(o_ref.dtype)


def _mm(a, wt_stack, layer, row0, n, tm, tn, name):
    m, k = a.shape
    r0 = row0 // tn
    return _call(
        _mm_kernel, grid=(m // tm, n // tn),
        in_specs=[pl.BlockSpec((tm, k), lambda i, j: (i, 0)),
                  pl.BlockSpec((None, tn, k), lambda i, j: (layer, j + r0, 0))],
        out_specs=pl.BlockSpec((tm, tn), lambda i, j: (i, j)),
        out_shape=jax.ShapeDtypeStruct((m, n), F32), name=name)(a, wt_stack)


def _mm_res_kernel(a_ref, w_ref, r_ref, o_ref):
    o_ref[...] = r_ref[...] + _dot(a_ref[...], w_ref[...].astype(BF16))


def _mm_res(a, w_stack, layer, res, tm, tn, name):
    m, k = a.shape
    n = w_stack.shape[-1]
    return _call(
        _mm_res_kernel, grid=(m // tm, n // tn),
        in_specs=[pl.BlockSpec((tm, k), lambda i, j: (i, 0)),
                  pl.BlockSpec((None, k, tn), lambda i, j: (layer, 0, j)),
                  pl.BlockSpec((tm, tn), lambda i, j: (i, j))],
        out_specs=pl.BlockSpec((tm, tn), lambda i, j: (i, j)),
        out_shape=jax.ShapeDtypeStruct((m, n), F32), name=name)(a, w_stack, res)


def _mm_res_norm_kernel(a_ref, w_ref, r_ref, g_ref, o_ref, h_ref, xs_ref, ss_ref, *, n_tiles, tn):
    j = pl.program_id(1)
    x = r_ref[...] + _dot(a_ref[...], w_ref[...].astype(BF16))
    o_ref[...] = x
    xs_ref[j] = x
    part = jnp.sum(x * x, axis=-1, keepdims=True)

    @pl.when(j == 0)
    def _():
        ss_ref[...] = part

    @pl.when(j > 0)
    def _():
        ss_ref[...] += part

    @pl.when(j == n_tiles - 1)
    def _():
        scale = lax.rsqrt(ss_ref[...] * (1.0 / (n_tiles * tn)) + EPS)
        for t in range(n_tiles):
            cols = slice(t * tn, (t + 1) * tn)
            h_ref[:, cols] = ((xs_ref[t] * scale) * g_ref[:, cols]).astype(h_ref.dtype)


def _mm_res_norm(a, w_stack, layer, res, g_row, tm, tn, name):
    m, k = a.shape
    n = w_stack.shape[-1]
    nt = n // tn
    return _call(
        functools.partial(_mm_res_norm_kernel, n_tiles=nt, tn=tn), grid=(m // tm, nt),
        in_specs=[pl.BlockSpec((tm, k), lambda i, j: (i, 0)),
                  pl.BlockSpec((None, k, tn), lambda i, j: (layer, 0, j)),
                  pl.BlockSpec((tm, tn), lambda i, j: (i, j)),
                  pl.BlockSpec((1, n), lambda i, j: (0, 0))],
        out_specs=[pl.BlockSpec((tm, tn), lambda i, j: (i, j)), pl.BlockSpec((tm, n), lambda i, j: (i, 0))],
        out_shape=[jax.ShapeDtypeStruct((m, n), F32), jax.ShapeDtypeStruct((m, n), BF16)],
        scratch_shapes=[pltpu.VMEM((nt, tm, tn), F32), pltpu.VMEM((tm, 1), F32)],
        name=name)(a, w_stack, res, g_row)


def _l2norm_heads(x, dk, scale):
    parts = []
    for i in range(x.shape[1] // dk):
        p = x[:, i * dk:(i + 1) * dk]
        p = p * lax.rsqrt(jnp.sum(p * p, axis=-1, keepdims=True) + EPS)
        parts.append(p * scale if scale != 1.0 else p)
    return jnp.concatenate(parts, axis=-1)


def _qkv_finish(c, o_write, n, n_q, n_k, dk):
    if n_q == 0 and n_k == 0:
        o_write(c)
        return

    @pl.when(n < n_q)
    def _():
        o_write(_l2norm_heads(c, dk, dk ** -0.5))

    @pl.when(jnp.logical_and(n >= n_q, n < n_q + n_k))
    def _():
        o_write(_l2norm_heads(c, dk, 1.0))

    @pl.when(n >= n_q + n_k)
    def _():
        o_write(c)


def _conv_rows(zp_ref, cw_ref, seq, taps, cols=slice(None)):
    acc = zp_ref[SUBLANES:SUBLANES + seq, :] * cw_ref[taps - 1:taps, cols]
    for i in range(taps - 1):
        s = taps - 1 - i
        acc = acc + zp_ref[SUBLANES - s:SUBLANES - s + seq, :] * cw_ref[i:i + 1, cols]
    return acc


TILES_PER_STEP = 2


def _qkv_prompt_kernel(h_ref, w_ref, cw_ref, qkv_ref, nb_ref, *zs_refs, seq, pad, taps, q_steps, l2, dk):
    j = pl.program_id(1)
    tn = MXU_COLS
    h = h_ref[...]
    qkv_ref[0:pad, :] = jnp.zeros((pad, qkv_ref.shape[1]), qkv_ref.dtype)
    scale = jnp.where(j < q_steps, dk ** -0.5, 1.0)
    for t, zs_ref in enumerate(zs_refs):
        cols = slice(t * tn, (t + 1) * tn)
        zs_ref[0:SUBLANES, :] = jnp.zeros((SUBLANES, tn), F32)
        zs_ref[SUBLANES:SUBLANES + seq, :] = _dot_nt(h, w_ref[cols, :])
        nb_ref[:, cols] = zs_ref[SUBLANES + seq - (taps - 1):SUBLANES + seq, :]
        c = _silu(_conv_rows(zs_ref, cw_ref, seq, taps, cols))
        if l2:
            c = _l2norm_heads(c, dk, 1.0) * scale
        qkv_ref[pad:pad + seq, cols] = c.astype(qkv_ref.dtype)


def _qkv_prompt(h3, w_t, layer, cw, pad, dk, col0, width, n_q, l2, out_dtype):
    b, seq, d = h3.shape
    taps = cw.shape[0]
    tn = MXU_COLS
    tw = TILES_PER_STEP * tn
    assert col0 % tw == 0 and width % tw == 0 and n_q % TILES_PER_STEP == 0
    c0 = col0 // tw
    kern = functools.partial(_qkv_prompt_kernel, seq=seq, pad=pad, taps=taps, q_steps=n_q // TILES_PER_STEP,
                             l2=l2, dk=dk)
    return _call(
        kern, grid=(b, width // tw),
        in_specs=[pl.BlockSpec((None, seq, d), lambda i, j: (i, 0, 0), pipeline_mode=pl.Buffered(1)),
                  pl.BlockSpec((None, tw, d), lambda i, j: (layer, j + c0, 0)),
                  pl.BlockSpec((taps, tw), lambda i, j: (0, j + c0))],
        out_specs=[pl.BlockSpec((None, seq + pad, tw), lambda i, j: (i, 0, j)),
                   pl.BlockSpec((None, taps - 1, tw), lambda i, j: (i, 0, j))],
        out_shape=[jax.ShapeDtypeStruct((b, seq + pad, width), out_dtype),
                   jax.ShapeDtypeStruct((b, taps - 1, width), F32)],
        scratch_shapes=[pltpu.VMEM((seq + SUBLANES, tn), F32)] * TILES_PER_STEP,
        name="qkv_prompt")(h3, w_t, cw)


def _qkv_sample_kernel(h_ref, w_ref, cw_ref, st_ref, qkv_ref, z_ref, *, taps, n_q, n_k, dk):
    n = pl.program_id(0)
    z = _dot_nt(h_ref[...], w_ref[...])
    z_ref[...] = z
    acc = z * cw_ref[taps - 1:taps, :]
    for i in range(taps - 1):
        acc = acc + st_ref[i] * cw_ref[i:i + 1, :]
    c = _silu(acc)

    def write(val):
        qkv_ref[...] = val

    _qkv_finish(c, write, n, n_q, n_k, dk)


def _qkv_sample(h2, w_stack, layer, cw, st_stack, dk, key_w):
    bs, d = h2.shape
    taps, width = cw.shape
    tn = MXU_COLS
    kern = functools.partial(_qkv_sample_kernel, taps=taps, n_q=key_w // tn, n_k=key_w // tn, dk=dk)
    return _call(
        kern, grid=(width // tn,),
        in_specs=[pl.BlockSpec((bs, d), lambda j: (0, 0)),
                  pl.BlockSpec((None, tn, d), lambda j: (layer, j, 0)),
                  pl.BlockSpec((taps, tn), lambda j: (0, j)),
                  pl.BlockSpec((None, taps - 1, bs, tn), lambda j: (layer, 0, 0, j))],
        out_specs=[pl.BlockSpec((bs, tn), lambda j: (0, j)), pl.BlockSpec((bs, tn), lambda j: (0, j))],
        out_shape=[jax.ShapeDtypeStruct((bs, width), F32), jax.ShapeDtypeStruct((bs, width), F32)],
        name="qkv_sample")(h2, w_stack, cw, st_stack)


def _gates_math(ab, alog_ref, dtb_ref):
    g = -jnp.exp(alog_ref[...]) * _softplus(ab + dtb_ref[...])
    return g, _sigmoid(ab)


def _expand_heads(x, first_head, lane_off, heads, dk):
    src = lax.broadcasted_iota(jnp.int32, (LANES, heads * dk), 0)
    dst = lax.broadcasted_iota(jnp.int32, (LANES, heads * dk), 1) // dk + (first_head + lane_off)
    onehot = jnp.where(src == dst, 1.0, 0.0).astype(BF16)
    out = None
    for _ in range(3):
        piece = x.astype(BF16)
        x = x - piece.astype(F32)
        term = _dot(piece, onehot)
        out = term if out is None else out + term
    return out


def _gates_prompt_kernel(h_ref, wab_ref, alog_ref, dtb_ref, g_ref, b_ref, gp_ref, bp_ref,
                         *, seq, pad, n_heads, dk, heads_per_step):
    j = pl.program_id(1)

    @pl.when(j == 0)
    def _():
        ab = _dot_nt(h_ref[...], wab_ref[...])
        g, beta = _gates_math(ab, alog_ref, dtb_ref)
        zeros = jnp.zeros((pad, LANES), F32)
        gp_ref[0:pad, :] = zeros
        bp_ref[0:pad, :] = zeros
        gp_ref[pad:pad + seq, :] = g
        bp_ref[pad:pad + seq, :] = beta
        row = lax.broadcasted_iota(jnp.int32, (CHUNK, CHUNK), 0)
        col = lax.broadcasted_iota(jnp.int32, (CHUNK, CHUNK), 1)
        tril = (row >= col).astype(F32)
        for c in range((seq + pad) // CHUNK):
            rows = slice(c * CHUNK, (c + 1) * CHUNK)
            gp_ref[rows, :] = _dot(tril, gp_ref[rows, :], precision=HIGHEST)

    g_ref[...] = _expand_heads(gp_ref[...], j * heads_per_step, 0, heads_per_step, dk)
    b_ref[...] = _expand_heads(bp_ref[...], j * heads_per_step, n_heads, heads_per_step, dk)


def _gates_prompt(h3, w_t, ab_row0, layer, alog_row, dtb_row, pad, n_heads, dk):
    assert ab_row0 % LANES == 0
    b, seq, d = h3.shape
    t = seq + pad
    hps = 4
    kern = functools.partial(_gates_prompt_kernel, seq=seq, pad=pad, n_heads=n_heads, dk=dk, heads_per_step=hps)
    out = jax.ShapeDtypeStruct((b, t, n_heads * dk), F32)
    return _call(
        kern, grid=(b, n_heads // hps),
        in_specs=[pl.BlockSpec((None, seq, d), lambda i, j: (i, 0, 0), pipeline_mode=pl.Buffered(1)),
                  pl.BlockSpec((None, LANES, d), lambda i, j: (layer, ab_row0 // LANES, 0)),
                  pl.BlockSpec((1, LANES), lambda i, j: (0, 0)),
                  pl.BlockSpec((1, LANES), lambda i, j: (0, 0))],
        out_specs=[pl.BlockSpec((None, t, hps * dk), lambda i, j: (i, 0, j)),
                   pl.BlockSpec((None, t, hps * dk), lambda i, j: (i, 0, j))],
        out_shape=[out, out],
        scratch_shapes=[pltpu.VMEM((t, LANES), F32), pltpu.VMEM((t, LANES), F32)],
        name="gates_prompt")(h3, w_t, alog_row, dtb_row)


def _gates_sample_kernel(h_ref, wab_ref, alog_ref, dtb_ref, g_ref, b_ref, *, n_heads, dk):
    ab = _dot_nt(h_ref[...], wab_ref[...])
    g, beta = _gates_math(ab, alog_ref, dtb_ref)
    g_ref[...] = _expand_heads(g, 0, 0, n_heads, dk)
    b_ref[...] = _expand_heads(beta, 0, n_heads, n_heads, dk)


def _gates_sample(h2, w_t, ab_row0, layer, alog_row, dtb_row, n_heads, dk):
    bs, d = h2.shape
    out = jax.ShapeDtypeStruct((bs, n_heads * dk), F32)
    return _call(
        functools.partial(_gates_sample_kernel, n_heads=n_heads, dk=dk), grid=(1,),
        in_specs=[pl.BlockSpec((bs, d), lambda i: (0, 0)),
                  pl.BlockSpec((None, LANES, d), lambda i: (layer, ab_row0 // LANES, 0)),
                  pl.BlockSpec((1, LANES), lambda i: (0, 0)),
                  pl.BlockSpec((1, LANES), lambda i: (0, 0))],
        out_specs=[pl.BlockSpec((bs, n_heads * dk), lambda i: (0, 0)),
                   pl.BlockSpec((bs, n_heads * dk), lambda i: (0, 0))],
        out_shape=[out, out], name="gates_sample")(h2, w_t, alog_row, dtb_row)


def _mixb_prompt_kernel(h_ref, wb_ref, wc_ref, wh_ref, cw_ref, y_ref, nb_ref, ps_ref, *, seq, taps):
    h = h_ref[...]
    pre = _dot_nt(h, wc_ref[...]) * _dot_nt(h, wh_ref[...])
    ps_ref[0:SUBLANES, :] = jnp.zeros((SUBLANES, pre.shape[1]), F32)
    ps_ref[SUBLANES:SUBLANES + seq, :] = pre
    nb_ref[...] = ps_ref[SUBLANES + seq - (taps - 1):SUBLANES + seq, :]
    acc = pre * cw_ref[taps - 1:taps, :]
    for i in range(taps - 1):
        s = taps - 1 - i
        acc = acc + ps_ref[SUBLANES - s:SUBLANES - s + seq, :] * cw_ref[i:i + 1, :]
    y_ref[...] = (_dot_nt(h, wb_ref[...]) * acc).astype(y_ref.dtype)


def _mixb_prompt(h3, w_stack, cols, layer, cw):
    b, seq, d = h3.shape
    taps, width = cw.shape
    tn = MXU_COLS

    def wspec(col0):
        return pl.BlockSpec((None, tn, d), lambda i, j: (layer, j + col0 // tn, 0))

    return _call(
        functools.partial(_mixb_prompt_kernel, seq=seq, taps=taps), grid=(b, width // tn),
        in_specs=[pl.BlockSpec((None, seq, d), lambda i, j: (i, 0, 0), pipeline_mode=pl.Buffered(1)),
                  wspec(cols[0]), wspec(cols[1]), wspec(cols[2]), pl.BlockSpec((taps, tn), lambda i, j: (0, j))],
        out_specs=[pl.BlockSpec((None, seq, tn), lambda i, j: (i, 0, j)),
                   pl.BlockSpec((None, taps - 1, tn), lambda i, j: (i, 0, j))],
        out_shape=[jax.ShapeDtypeStruct((b, seq, width), BF16),
                   jax.ShapeDtypeStruct((b, taps - 1, width), F32)],
        scratch_shapes=[pltpu.VMEM((seq + SUBLANES, tn), F32)],
        name="mixb_prompt")(h3, w_stack, w_stack, w_stack, cw)


def _mixb_sample_kernel(h_ref, wb_ref, wc_ref, wh_ref, cw_ref, st_ref, y_ref, pre_ref, *, taps):
    h = h_ref[...]
    pre = _dot_nt(h, wc_ref[...]) * _dot_nt(h, wh_ref[...])
    pre_ref[...] = pre
    acc = pre * cw_ref[taps - 1:taps, :]
    for i in range(taps - 1):
        acc = acc + st_ref[i] * cw_ref[i:i + 1, :]
    y_ref[...] = (_dot_nt(h, wb_ref[...]) * acc).astype(y_ref.dtype)


def _mixb_sample(h2, w_stack, cols, layer, cw, st_stack):
    bs, d = h2.shape
    taps, width = cw.shape
    tn = MXU_COLS

    def wspec(col0):
        return pl.BlockSpec((None, tn, d), lambda j: (layer, j + col0 // tn, 0))

    return _call(
        functools.partial(_mixb_sample_kernel, taps=taps), grid=(width // tn,),
        in_specs=[pl.BlockSpec((bs, d), lambda j: (0, 0)), wspec(cols[0]), wspec(cols[1]), wspec(cols[2]),
                  pl.BlockSpec((taps, tn), lambda j: (0, j)),
                  pl.BlockSpec((None, taps - 1, bs, tn), lambda j: (layer, 0, 0, j))],
        out_specs=[pl.BlockSpec((bs, tn), lambda j: (0, j)), pl.BlockSpec((bs, tn), lambda j: (0, j))],
        out_shape=[jax.ShapeDtypeStruct((bs, width), BF16), jax.ShapeDtypeStruct((bs, width), F32)],
        name="mixb_sample")(h2, w_stack, w_stack, w_stack, cw, st_stack)


def _bmm(a, b):
    return jnp.einsum("cij,cjk->cik", a.astype(BF16), b.astype(BF16), preferred_element_type=F32)


def _bmm_nt(a, b):
    return jnp.einsum("cid,cjd->cij", a.astype(BF16), b.astype(BF16), preferred_element_type=F32)


def _unit_lower_inverse(lm, row, col, eye):
    base = 16
    in_blk = (row // base) == (col // base)
    ld = jnp.where(in_blk, lm, 0.0)
    x = eye.astype(F32) - ld
    m = ld
    for _ in range(3):
        m = _bmm(m, m)
        x = x + _bmm(x, m)
    size = base
    while size < CHUNK:
        size *= 2
        in_big = (row // size) == (col // size)
        off = jnp.where(jnp.logical_and(in_big, jnp.logical_not(in_blk)), lm, 0.0)
        x = x - _bmm(x, _bmm(off, x))
        in_blk = in_big
    return x


HEADS_PER_STEP = 2


def _delta_prompt_kernel(q_ref, k_ref, v_ref, g_ref, b_ref, go_ref, og_ref, o_ref, s_ref,
                         w_s, u_s, kt_s, qk_s, oo_s, *, seq, pad, dk):
    t = seq + pad
    nc = t // CHUNK
    row = lax.broadcasted_iota(jnp.int32, (CHUNK, CHUNK), 0)
    col = lax.broadcasted_iota(jnp.int32, (CHUNK, CHUNK), 1)
    eye = row == col
    incl = row >= col
    for hh in range(HEADS_PER_STEP):
        lanes = slice(hh * dk, (hh + 1) * dk)
        q3 = q_ref[:, lanes].reshape(nc, CHUNK, dk)
        k3 = k_ref[:, lanes].reshape(nc, CHUNK, dk)
        v3 = v_ref[:, lanes].reshape(nc, CHUNK, dk)
        g3 = g_ref[:, lanes].reshape(nc, CHUNK, dk)
        b3 = b_ref[:, lanes].reshape(nc, CHUNK, dk)
        kf = k3.astype(F32)
        gi = g3[:, :, :CHUNK]
        gj = jnp.sum(jnp.where(eye, gi, 0.0), axis=1, keepdims=True)
        decay = jnp.where(incl, jnp.exp(jnp.where(incl, gi - gj, 0.0)), 0.0)
        lm = jnp.where(row > col, b3[:, :, :CHUNK] * _bmm_nt(k3, k3) * decay, 0.0)
        tinv = _unit_lower_inverse(lm, row, col, eye)
        u_s[hh] = _bmm(tinv, b3 * v3).reshape(t, dk)
        w_s[hh] = _bmm(tinv, b3 * jnp.exp(g3) * kf).reshape(t, dk).astype(BF16)
        qk_s[hh] = (_bmm_nt(q3, k3) * decay).reshape(t, CHUNK).astype(BF16)
        kt_s[hh] = (kf * jnp.exp(g3[:, CHUNK - 1:CHUNK, :] - g3)).reshape(t, dk).astype(BF16)

    def body(c, states):
        r0 = pl.multiple_of(c * CHUNK, CHUNK)
        rows = pl.ds(r0, CHUNK)
        new_states = []
        for hh in range(HEADS_PER_STEP):
            lanes = slice(hh * dk, (hh + 1) * dk)
            s = states[hh]
            sb = s.astype(BF16)
            u = u_s[hh, rows, :] - _dot(w_s[hh, rows, :], sb)
            ub = u.astype(BF16)
            oo_s[hh, rows, :] = (jnp.exp(g_ref[rows, lanes]) * _dot(q_ref[rows, lanes], sb)
                                 + _dot(qk_s[hh, rows, :], ub))
            g_tail = g_ref[pl.ds(pl.multiple_of(r0 + CHUNK - SUBLANES, SUBLANES), SUBLANES), lanes]
            s_scale = jnp.exp(g_tail[SUBLANES - 1:SUBLANES, :])
            new_states.append(s_scale * s + lax.dot_general(kt_s[hh, rows, :], ub, (((0,), (0,)), ((), ())),
                                                            preferred_element_type=F32))
        return tuple(new_states)

    init = tuple(jnp.zeros((dk, dk), F32) for _ in range(HEADS_PER_STEP))
    states = lax.fori_loop(0, nc, body, init)
    for hh in range(HEADS_PER_STEP):
        lanes = slice(hh * dk, (hh + 1) * dk)
        s_ref[hh] = states[hh]
        o = oo_s[hh, pad:pad + seq, :]
        o = (o * lax.rsqrt(jnp.mean(o * o, axis=-1, keepdims=True) + EPS)) * og_ref[...]
        o_ref[:, lanes] = (o * _silu(go_ref[:, lanes])).astype(o_ref.dtype)


def _delta_prompt(qk, v, gcb, bcb, gout3, og_row, pad, n_heads, dk):
    b, t, _ = v.shape
    seq = t - pad
    hps = HEADS_PER_STEP
    wd = hps * dk

    def col_spec(rows, off):
        return pl.BlockSpec((None, rows, wd), lambda i, j: (i, 0, j + off))

    kern = functools.partial(_delta_prompt_kernel, seq=seq, pad=pad, dk=dk)
    return _call(
        kern, grid=(b, n_heads // hps),
        in_specs=[col_spec(t, 0), col_spec(t, n_heads // hps), col_spec(t, 0),
                  col_spec(t, 0), col_spec(t, 0), col_spec(seq, 0),
                  pl.BlockSpec((1, dk), lambda i, j: (0, 0))],
        out_specs=[col_spec(seq, 0), pl.BlockSpec((None, hps, dk, dk), lambda i, j: (i, j, 0, 0))],
        out_shape=[jax.ShapeDtypeStruct((b, seq, n_heads * dk), BF16),
                   jax.ShapeDtypeStruct((b, n_heads, dk, dk), F32)],
        scratch_shapes=[pltpu.VMEM((hps, t, dk), BF16), pltpu.VMEM((hps, t, dk), F32),
                        pltpu.VMEM((hps, t, dk), BF16), pltpu.VMEM((hps, t, CHUNK), BF16),
                        pltpu.VMEM((hps, t, dk), F32)],
        name="delta_prompt")(qk, qk, v, gcb, bcb, gout3, og_row)


def _delta_sample_kernel(q_ref, k_ref, v_ref, g_ref, b_ref, go_ref, og_ref, s_ref, *rest, n_heads, dk):
    o_ref, sn_ref = rest[-2:]
    for bi in range(SAMPLES_PER_STEP):
        q = q_ref[bi]
        k = k_ref[bi]
        v = v_ref[bi]
        kq = jnp.concatenate([k, q, jnp.zeros((LANES - 2 * n_heads, dk), F32)], axis=0)
        kq_t = kq.T
        qk = jnp.sum(q * k, axis=-1, keepdims=True)
        outs = []
        for h in range(n_heads):
            s = s_ref[bi, h]
            a = jnp.exp(g_ref[bi, :, h * dk:(h + 1) * dk])
            beta = b_ref[bi, :, h * dk:(h + 1) * dk]
            kcol = kq_t[:, h:h + 1]
            qcol = kq_t[:, n_heads + h:n_heads + h + 1]
            ks = jnp.sum(kcol * s, axis=0, keepdims=True)
            qs = jnp.sum(qcol * s, axis=0, keepdims=True)
            u = beta * (v[h:h + 1, :] - a * ks)
            outs.append(a * qs + qk[h:h + 1, :] * u)
            sn_ref[bi, h] = a * s + kcol * u
        o = jnp.concatenate(outs, axis=0)
        o = (o * lax.rsqrt(jnp.mean(o * o, axis=-1, keepdims=True) + EPS)) * og_ref[...]
        o_ref[bi] = (o * _silu(go_ref[bi])).astype(o_ref.dtype)


SAMPLES_PER_STEP = 4


def _delta_sample(q3, k3, v3, gcb, bcb, gout3, og_row, state, layer, prev_out):
    bs, n_heads, dk = q3.shape
    depth = state.shape[0]
    nb = SAMPLES_PER_STEP
    hspec = pl.BlockSpec((nb, n_heads, dk), lambda i: (i, 0, 0))
    rspec = pl.BlockSpec((nb, 1, n_heads * dk), lambda i: (i, 0, 0))
    sspec = pl.BlockSpec((None, nb, n_heads, dk, dk), lambda i: (layer, i, 0, 0, 0))
    in_specs = [hspec, hspec, hspec, rspec, rspec, hspec, pl.BlockSpec((1, dk), lambda i: (0, 0)), sspec]
    args = [q3, k3, v3, gcb.reshape(bs, 1, -1), bcb.reshape(bs, 1, -1), gout3, og_row, state]
    aliases = {}
    if prev_out is not None:
        aliases = {len(args): 1}
        in_specs.append(pl.BlockSpec(memory_space=pl.ANY))
        args.append(prev_out)
    return _call(
        functools.partial(_delta_sample_kernel, n_heads=n_heads, dk=dk), grid=(bs // nb,),
        in_specs=in_specs, out_specs=[hspec, sspec],
        out_shape=[jax.ShapeDtypeStruct((bs, n_heads, dk), BF16),
                   jax.ShapeDtypeStruct((depth, bs, n_heads, dk, dk), F32)],
        input_output_aliases=aliases, name="delta_sample")(*args)


def _merge_kernel(o_ref, y_ref, h_ref, wa_ref, wb_ref, wza_ref, wzb_ref, m_ref):
    h = h_ref[...]
    za = _sigmoid(_dot_nt(h, wza_ref[...]))
    zb = _sigmoid(_dot_nt(h, wzb_ref[...]))
    pa = _dot(o_ref[...], wa_ref[...].astype(BF16))
    pb = _dot(y_ref[...], wb_ref[...].astype(BF16))
    m_ref[...] = (za * pa + zb * pb).astype(m_ref.dtype)


def _merge(o2, y2, h2, wa_stack, wb_stack, wz_stack, col_za, col_zb, layer, tm):
    m, d = h2.shape
    ka = o2.shape[1]
    kb = y2.shape[1]
    n = wa_stack.shape[-1]
    tn = MXU_COLS

    def wspec(kk):
        return pl.BlockSpec((None, kk, tn), lambda i, j: (layer, 0, j))

    def wtspec(row0):
        return pl.BlockSpec((None, tn, d), lambda i, j: (layer, j + row0 // tn, 0))

    return _call(
        _merge_kernel, grid=(m // tm, n // tn),
        in_specs=[pl.BlockSpec((tm, ka), lambda i, j: (i, 0)), pl.BlockSpec((tm, kb), lambda i, j: (i, 0)),
                  pl.BlockSpec((tm, d), lambda i, j: (i, 0)), wspec(ka), wspec(kb),
                  wtspec(col_za), wtspec(col_zb)],
        out_specs=pl.BlockSpec((tm, tn), lambda i, j: (i, j)),
        out_shape=jax.ShapeDtypeStruct((m, n), BF16), name="merge")(
            o2, y2, h2, wa_stack, wb_stack, wz_stack, wz_stack)


def _ffn_prompt_kernel(h_ref, wg_ref, wv_ref, cw_ref, cb_ref, a_ref, nb_ref, gs_ref, *, seq, taps):
    h = h_ref[...]
    gs_ref[0:SUBLANES, :] = jnp.zeros((SUBLANES, gs_ref.shape[1]), F32)
    gs_ref[SUBLANES:SUBLANES + seq, :] = _dot(h, wg_ref[...].astype(BF16))
    nb_ref[...] = gs_ref[SUBLANES + seq - (taps - 1):SUBLANES + seq, :]
    acc = _conv_rows(gs_ref, cw_ref, seq, taps)
    a_ref[...] = (_silu(acc + cb_ref[...]) * _dot(h, wv_ref[...].astype(BF16))).astype(a_ref.dtype)


def _ffn_prompt(h3, wup_stack, layer, cw, cb_row):
    b, seq, d = h3.shape
    dff = cw.shape[1]
    taps = cw.shape[0]
    tn = MXU_COLS
    nt = dff // tn
    return _call(
        functools.partial(_ffn_prompt_kernel, seq=seq, taps=taps), grid=(b, nt),
        in_specs=[pl.BlockSpec((None, seq, d), lambda i, j: (i, 0, 0), pipeline_mode=pl.Buffered(1)),
                  pl.BlockSpec((None, d, tn), lambda i, j: (layer, 0, j)),
                  pl.BlockSpec((None, d, tn), lambda i, j: (layer, 0, j + nt)),
                  pl.BlockSpec((taps, tn), lambda i, j: (0, j)),
                  pl.BlockSpec((1, tn), lambda i, j: (0, j))],
        out_specs=[pl.BlockSpec((None, seq, tn), lambda i, j: (i, 0, j)),
                   pl.BlockSpec((None, taps - 1, tn), lambda i, j: (i, 0, j))],
        out_shape=[jax.ShapeDtypeStruct((b, seq, dff), BF16),
                   jax.ShapeDtypeStruct((b, taps - 1, dff), F32)],
        scratch_shapes=[pltpu.VMEM((seq + SUBLANES, tn), F32)],
        name="ffn_prompt")(h3, wup_stack, wup_stack, cw, cb_row)


def _ffn_sample_kernel(h_ref, wg_ref, wv_ref, cw_ref, cb_ref, st_ref, a_ref, g_ref, *, taps):
    h = h_ref[...]
    gate = _dot(h, wg_ref[...].astype(BF16))
    g_ref[...] = gate
    acc = gate * cw_ref[taps - 1:taps, :]
    for i in range(taps - 1):
        acc = acc + st_ref[i] * cw_ref[i:i + 1, :]
    a_ref[...] = (_silu(acc + cb_ref[...]) * _dot(h, wv_ref[...].astype(BF16))).astype(a_ref.dtype)


def _ffn_sample(h2, wup_stack, layer, cw, cb_row, st_stack):
    bs, d = h2.shape
    dff = cw.shape[1]
    taps = cw.shape[0]
    tn = MXU_COLS
    nt = dff // tn
    return _call(
        functools.partial(_ffn_sample_kernel, taps=taps), grid=(nt,),
        in_specs=[pl.BlockSpec((bs, d), lambda j: (0, 0)),
                  pl.BlockSpec((None, d, tn), lambda j: (layer, 0, j)),
                  pl.BlockSpec((None, d, tn), lambda j: (layer, 0, j + nt)),
                  pl.BlockSpec((taps, tn), lambda j: (0, j)),
                  pl.BlockSpec((1, tn), lambda j: (0, j)),
                  pl.BlockSpec((None, taps - 1, bs, tn), lambda j: (layer, 0, 0, j))],
        out_specs=[pl.BlockSpec((bs, tn), lambda j: (0, j)), pl.BlockSpec((bs, tn), lambda j: (0, j))],
        out_shape=[jax.ShapeDtypeStruct((bs, dff), BF16), jax.ShapeDtypeStruct((bs, dff), F32)],
        name="ffn_sample")(h2, wup_stack, wup_stack, cw, cb_row, st_stack)


def _cast_kernel(x_ref, o_ref):
    o_ref[...] = x_ref[0].astype(o_ref.dtype)


def _cast_rows_bf16(w_t, row0, rows):
    depth, _, k = w_t.shape
    tr = 512
    assert row0 % 16 == 0 and rows % tr == 0
    return _call(
        _cast_kernel, grid=(depth, rows // tr),
        in_specs=[pl.BlockSpec((pl.Element(1), pl.Element(tr), pl.Element(k)),
                               lambda l, i: (l, pl.multiple_of(row0 + i * tr, 16), 0))],
        out_specs=pl.BlockSpec((None, tr, k), lambda l, i: (l, i, 0)),
        out_shape=jax.ShapeDtypeStruct((depth, rows, k), BF16), name="cast_rows")(w_t)


def _pad_lanes(x2d):
    return jnp.pad(x2d, ((0, 0), (0, LANES - x2d.shape[1])))


def _roll_state(state_l, new_row):
    return jnp.concatenate([state_l[:, 1:], new_row[:, None]], axis=1)


def kernel(x_prompt, x_sample, state_delta, state_conv_qkv, state_conv_mix, state_conv_ffn, meta_tokens,
           norm1_g, w_in, conv_qkv_w, a_log, dt_bias, onorm_g, conv_mix_w, w_up_a, w_up_b, w_o, norm2_g,
           w_ffn_up, conv_ffn_w, conv_ffn_b, w_ffn_down, final_g):
    b, seq0, d = x_prompt.shape
    bs = x_sample.shape[0]
    depth, _, n_heads, dk, dv = state_delta.shape
    assert dk == dv == LANES and x_sample.shape[1] == 1
    n_meta = meta_tokens.shape[0]
    seq = seq0 + n_meta
    pad = (-seq) % CHUNK
    key_w = n_heads * dk
    val_w = n_heads * dv
    qkv_w = state_conv_qkv.shape[-1]
    w_b = state_conv_mix.shape[-1]
    d_ff = state_conv_ffn.shape[-1]
    assert 2 * n_heads <= LANES

    o_ab = qkv_w
    o_go = o_ab + 2 * n_heads
    o_bg = o_go + val_w
    o_cg = o_bg + w_b
    o_hb = o_cg + w_b
    o_za = o_hb + w_b
    o_zb = o_za + d
    assert o_zb + d == w_in.shape[-1]

    w_t = jnp.swapaxes(w_in, 1, 2)
    w_rest = _cast_rows_bf16(w_t, o_go, w_in.shape[-1] - o_go)
    c_go, c_bg, c_cg, c_hb, c_za, c_zb = (o - o_go for o in (o_go, o_bg, o_cg, o_hb, o_za, o_zb))
    wua, wub = w_up_a, w_up_b
    wo, wfd = w_o.astype(BF16), w_ffn_down.astype(BF16)
    st_qkv = jnp.swapaxes(state_conv_qkv, 1, 2)
    st_mix = jnp.swapaxes(state_conv_mix, 1, 2)
    st_ffn = jnp.swapaxes(state_conv_ffn, 1, 2)

    xp = jnp.concatenate([jnp.broadcast_to(meta_tokens[None], (b, n_meta, d)), x_prompt], axis=1)
    xp = xp.reshape(b * seq, d)
    xs = x_sample.reshape(bs, d)
    m = b * seq
    tm = seq // 3
    rows_norm = tm
    assert seq % 3 == 0 and tm % 16 == 0
    tn_o = 512
    tn_f = MXU_COLS

    sp, qp, qs, mp, ms, fp, fs = [], [], [], [], [], [], []
    s_sample = jnp.zeros(state_delta.shape, F32)
    for l in range(depth):
        alog_row = _pad_lanes(a_log[l][None])
        dtb_row = _pad_lanes(dt_bias[l][None])
        og_row = onorm_g[l][None]
        cb_row = conv_ffn_b[l][None]

        h = _rmsnorm(xp, norm1_g[l][None], BF16, rows_norm)
        h3 = h.reshape(b, seq, d)
        qk, nb_qk = _qkv_prompt(h3, w_t, l, conv_qkv_w[l], pad, dk, 0, 2 * key_w, key_w // MXU_COLS, True, BF16)
        v, nb_v = _qkv_prompt(h3, w_t, l, conv_qkv_w[l], pad, dk, 2 * key_w, val_w, 0, False, F32)
        nb_q = jnp.concatenate([nb_qk, nb_v], axis=-1)
        gcb, bcb = _gates_prompt(h3, w_t, o_ab, l, alog_row, dtb_row, pad, n_heads, dk)
        gout = _mm(h, w_rest, l, c_go, val_w, tm, 1024, "gout")
        y_b, nb_m = _mixb_prompt(h3, w_rest, (c_bg, c_cg, c_hb), l, conv_mix_w[l])
        o, s_new = _delta_prompt(qk, v, gcb, bcb, gout.reshape(b, seq, val_w), og_row, pad, n_heads, dk)
        merged = _merge(o.reshape(m, val_w), y_b.reshape(m, w_b), h, wua, wub, w_rest, c_za, c_zb, l, tm)
        xp, h2 = _mm_res_norm(merged, wo, l, xp, norm2_g[l][None], tm, tn_o, "out_proj")
        act, nb_f = _ffn_prompt(h2.reshape(b, seq, d), w_ffn_up, l, conv_ffn_w[l], cb_row)
        xp = _mm_res(act.reshape(m, d_ff), wfd, l, xp, tm, tn_f, "ffn_down")
        sp.append(s_new); qp.append(nb_q); mp.append(nb_m); fp.append(nb_f)

        hs = _rmsnorm(xs, norm1_g[l][None], BF16, bs)
        qkv_s, z_new = _qkv_sample(hs, w_t, l, conv_qkv_w[l], st_qkv, dk, key_w)
        gcb_s, bcb_s = _gates_sample(hs, w_t, o_ab, l, alog_row, dtb_row, n_heads, dk)
        gout_s = _mm(hs, w_rest, l, c_go, val_w, bs, 1024, "gout")
        yb_s, pre_new = _mixb_sample(hs, w_rest, (c_bg, c_cg, c_hb), l, conv_mix_w[l], st_mix)
        q3 = qkv_s[:, :key_w].reshape(bs, n_heads, dk)
        k3 = qkv_s[:, key_w:2 * key_w].reshape(bs, n_heads, dk)
        v3 = qkv_s[:, 2 * key_w:].reshape(bs, n_heads, dv)
        o_s, s_sample = _delta_sample(q3, k3, v3, gcb_s, bcb_s, gout_s.reshape(bs, n_heads, dv), og_row,
                                      state_delta, l, s_sample)
        merged_s = _merge(o_s.reshape(bs, val_w), yb_s, hs, wua, wub, w_rest, c_za, c_zb, l, bs)
        xs, h2s = _mm_res_norm(merged_s, wo, l, xs, norm2_g[l][None], bs, tn_o, "out_proj")
        act_s, gate_new = _ffn_sample(h2s, w_ffn_up, l, conv_ffn_w[l], cb_row, st_ffn)
        xs = _mm_res(act_s, wfd, l, xs, bs, tn_f, "ffn_down")
        qs.append(_roll_state(state_conv_qkv[l], z_new))
        ms.append(_roll_state(state_conv_mix[l], pre_new))
        fs.append(_roll_state(state_conv_ffn[l], gate_new))

    y_prompt = _rmsnorm(xp, final_g[None], F32, rows_norm // 2).reshape(b, seq, d)[:, n_meta:]
    y_sample = _rmsnorm(xs, final_g[None], F32, bs).reshape(bs, 1, d)
    return (y_prompt, y_sample, jnp.stack(sp), s_sample, jnp.stack(qp), jnp.stack(qs),
            jnp.stack(mp), jnp.stack(ms), jnp.stack(fp), jnp.stack(fs))
```
